```python
import jax, jax.numpy as jnp
from jax import lax
import numpy as np

D_MODEL = 1024
BATCH = 1
SEQ = 16384
DEPTH = 1
DEC_BATCH = 8
DEC_SEQ = 4096
PAST_LEN = 128

ML_HEADS = 4
ML_DQK = 128
ML_DV = 256
ML_QK_W = ML_HEADS * ML_DQK
ML_V_W = ML_HEADS * ML_DV
ML_CHUNK = 128
ML_SPLITS = (ML_QK_W, ML_QK_W, ML_V_W, ML_V_W, ML_HEADS, ML_HEADS, ML_HEADS, ML_HEADS)
ML_COLS = 2 * ML_QK_W + 2 * ML_V_W + 4 * ML_HEADS
RW_HEAD = 64
RW_HEADS = D_MODEL // RW_HEAD
RW_W = RW_HEADS * RW_HEAD
RW_DECAY_LORA = 64
RW_AAA_LORA = 64
RW_GATE_LORA = 128
RW_SPLITS = (RW_W, RW_W, RW_W, RW_DECAY_LORA, RW_DECAY_LORA, RW_AAA_LORA, RW_GATE_LORA)
RW_COLS = 3 * RW_W + 2 * RW_DECAY_LORA + RW_AAA_LORA + RW_GATE_LORA
N_BRANCH = 2
GATE_COLS = N_BRANCH * D_MODEL
D_IN = ML_COLS + RW_COLS + GATE_COLS
N_GROUPS = 8
EXPERTS_PER_GROUP = 8
N_EXPERTS = N_GROUPS * EXPERTS_PER_GROUP
TOP_K = 2
D_EXPERT = 512
MOE_BLOCK = 128
NORM_EPS = 1e-6
RW_GN_EPS = 64e-5
L2_EPS = 1e-12

kernel_name = 'hybrid_mlstm_rwkv7_hmoe_encoder'


def rms_norm(x, g):
    xf = x.astype(jnp.float32)
    y = xf * lax.rsqrt(jnp.mean(xf * xf, axis=-1, keepdims=True) + NORM_EPS)
    return (y * g.astype(jnp.float32)).astype(x.dtype)


def split_cols(x, sizes):
    out, start = [], 0
    for s in sizes:
        out.append(x[..., start:start + s])
        start += s
    return out


def mlstm_chunkwise(q, k, v, ig, lf):
    B, T, H, DK = q.shape
    DV = v.shape[-1]
    nc = T // ML_CHUNK

    def to_chunks(a):
        a = a.reshape((B, nc, ML_CHUNK, H) + a.shape[3:])
        return jnp.moveaxis(a, (1, 3), (0, 2))

    tril = jnp.tril(jnp.ones((ML_CHUNK, ML_CHUNK), dtype=bool))

    def step(carry, chunk):
        C, n, m = carry
        qc, kc, vc, ic, fc = chunk
        b = jnp.cumsum(fc, axis=-1)
        logw = b[..., :, None] - b[..., None, :] + ic[..., None, :]
        logw = jnp.where(tril, logw, -jnp.inf)
        m_inter = b + m[..., None]
        m_t = jnp.maximum(m_inter, jnp.max(logw, axis=-1))
        s = jnp.einsum('bhtd,bhsd->bhts', qc, kc) * jnp.exp(logw - m_t[..., None])
        inter = jnp.exp(m_inter - m_t)
        num = jnp.einsum('bhts,bhsv->bhtv', s, vc) + inter[..., None] * jnp.einsum('bhtd,bhdv->bhtv', qc, C)
        den = jnp.sum(s, axis=-1) + inter * jnp.einsum('bhtd,bhd->bht', qc, n)
        h = num / jnp.maximum(jnp.abs(den), jnp.exp(-m_t))[..., None]
        b_last = b[..., -1]
        logu = b_last[..., None] - b + ic
        m_new = jnp.maximum(b_last + m, jnp.max(logu, axis=-1))
        decay = jnp.exp(b_last + m - m_new)
        u = jnp.exp(logu - m_new[..., None])
        C = decay[..., None, None] * C + jnp.einsum('bhs,bhsd,bhsv->bhdv', u, kc, vc)
        n = decay[..., None] * n + jnp.einsum('bhs,bhsd->bhd', u, kc)
        return (C, n, m_new), h

    f32 = jnp.float32
    init = (jnp.zeros((B, H, DK, DV), f32), jnp.zeros((B, H, DK), f32), jnp.zeros((B, H), f32))
    xs = (to_chunks(q), to_chunks(k), to_chunks(v), to_chunks(ig), to_chunks(lf))
    _, h = lax.scan(step, init, xs)
    return jnp.moveaxis(h, (0, 2), (1, 3)).reshape(B, T, H, DV)


def mlstm_branch(cols, b_i, b_f, norm_g):
    B, T, _ = cols.shape
    f32 = jnp.float32
    q, k, v, o, i_fwd, i_bwd, f_fwd, f_bwd = split_cols(cols.astype(f32), ML_SPLITS)
    q = q.reshape(B, T, ML_HEADS, ML_DQK) * (ML_DQK ** -0.5)
    k = k.reshape(B, T, ML_HEADS, ML_DQK)
    v = v.reshape(B, T, ML_HEADS, ML_DV)
    b_i = b_i.astype(f32)
    b_f = b_f.astype(f32)
    h_fwd = mlstm_chunkwise(q, k, v, i_fwd + b_i[0], jax.nn.log_sigmoid(f_fwd + b_f[0]))
    rev = lambda t: jnp.flip(t, axis=1)
    h_bwd = rev(mlstm_chunkwise(rev(q), rev(k), rev(v), rev(i_bwd + b_i[1]), rev(jax.nn.log_sigmoid(f_bwd + b_f[1]))))
    h = h_fwd + h_bwd
    h = h * lax.rsqrt(jnp.mean(h * h, axis=-1, keepdims=True) + NORM_EPS)
    h = h.reshape(B, T, ML_V_W) * norm_g.astype(f32)
    return (jax.nn.sigmoid(o) * h).astype(cols.dtype)


def token_shift(p, mu):
    zeros = jnp.zeros_like(p[:, :1])
    prev = jnp.concatenate([zeros, p[:, :-1]], axis=1)
    nxt = jnp.concatenate([p[:, 1:], zeros], axis=1)
    return p + mu * (0.5 * (prev + nxt) - p)


def rwkv7_scan(r, w, k, v, a, b, reverse):
    B, T, H, N = r.shape

    def step(S, inp):
        rt, wt, kt, vt, at, bt = inp
        sa = jnp.einsum('bhij,bhj->bhi', S, at)
        S = S * wt[:, :, None, :] + sa[..., :, None] * bt[..., None, :] + vt[..., :, None] * kt[..., None, :]
        return S, jnp.einsum('bhij,bhj->bhi', S, rt)

    xs = tuple(jnp.moveaxis(t, 1, 0) for t in (r, w, k, v, a, b))
    _, out = lax.scan(step, jnp.zeros((B, H, N, N), jnp.float32), xs, reverse=reverse)
    return jnp.moveaxis(out, 0, 1)


def rwkv7_branch(cols, mu, w0, w2, a0, a2, g2, k_k, k_a, r_k, gn_g, gn_b):
    B, T, _ = cols.shape
    f32 = jnp.float32
    c = token_shift(cols.astype(f32), mu.astype(f32))
    r, k, v, wd_f, wd_b, ad, gd = split_cols(c, RW_SPLITS)

    def decay(wd, w0_d, w2_d):
        wl = w0_d.astype(f32) + jnp.tanh(wd) @ w2_d.astype(f32)
        return jnp.exp(-jnp.exp(-jax.nn.softplus(-wl) - 0.5))

    w_fwd = decay(wd_f, w0[0], w2[0])
    w_bwd = decay(wd_b, w0[1], w2[1])
    a = jax.nn.sigmoid(a0.astype(f32) + ad @ a2.astype(f32))
    g = jax.nn.sigmoid(gd) @ g2.astype(f32)
    heads = lambda t: t.reshape(B, T, RW_HEADS, RW_HEAD)
    kk = heads(k * k_k.astype(f32))
    kk = kk / jnp.maximum(jnp.sqrt(jnp.sum(kk * kk, axis=-1, keepdims=True)), L2_EPS)
    k = k * (1.0 + (a - 1.0) * k_a.astype(f32))
    rh, kh, vh, ah = heads(r), heads(k), heads(v), heads(a)
    a_vec, b_vec = -kk, kk * ah
    out = (rwkv7_scan(rh, heads(w_fwd), kh, vh, a_vec, b_vec, False)
           + rwkv7_scan(rh, heads(w_bwd), kh, vh, a_vec, b_vec, True))
    mean = jnp.mean(out, axis=-1, keepdims=True)
    var = jnp.mean((out - mean) ** 2, axis=-1, keepdims=True)
    out = ((out - mean) * lax.rsqrt(var + RW_GN_EPS)).reshape(B, T, RW_W) * gn_g.astype(f32) + gn_b.astype(f32)
    bonus = (jnp.sum(rh * kh * r_k.astype(f32), axis=-1, keepdims=True) * vh).reshape(B, T, RW_W)
    return ((out + bonus) * g).astype(cols.dtype)


def hier_moe(x, w_rg, b_rg, w_re, b_re, w_gate, w_up, w_down):
    B, T, D = x.shape
    N = B * T
    f32 = jnp.float32
    xt = x.reshape(N, D)
    p_group = jax.nn.softmax((xt @ w_rg).astype(f32) + b_rg.astype(f32), axis=-1)
    p_g, g_idx = lax.top_k(p_group, 1)
    e_logits = ((xt @ w_re).astype(f32) + b_re.astype(f32)).reshape(N, N_GROUPS, EXPERTS_PER_GROUP)
    e_logits = jnp.einsum('ng,nge->ne', jax.nn.one_hot(g_idx[:, 0], N_GROUPS, dtype=f32), e_logits)
    p_e, e_idx = lax.top_k(jax.nn.softmax(e_logits, axis=-1), TOP_K)
    gates = p_g * p_e / jnp.sum(p_e, axis=-1, keepdims=True)
    expert_id = (g_idx * EXPERTS_PER_GROUP + e_idx).reshape(-1)
    A = N * TOP_K
    n_blocks = -(-A // MOE_BLOCK) + N_EXPERTS
    P = n_blocks * MOE_BLOCK
    token_id = jnp.arange(A, dtype=jnp.int32) // TOP_K
    order = jnp.argsort(expert_id)
    e_sorted = expert_id[order]
    counts = jnp.zeros((N_EXPERTS,), jnp.int32).at[expert_id].add(1)
    padded = ((counts + MOE_BLOCK - 1) // MOE_BLOCK) * MOE_BLOCK
    start = jnp.cumsum(counts) - counts
    pend = jnp.cumsum(padded)
    pstart = pend - padded
    dest = pstart[e_sorted] + (jnp.arange(A, dtype=jnp.int32) - start[e_sorted])
    buf_tok = jnp.full((P,), N, jnp.int32).at[dest].set(token_id[order])
    buf_w = jnp.zeros((P,), f32).at[dest].set(gates.reshape(-1)[order])
    block_start = jnp.arange(n_blocks, dtype=jnp.int32) * MOE_BLOCK
    block_e = jnp.minimum(jnp.searchsorted(pend, block_start, side='right'), N_EXPERTS - 1).astype(jnp.int32)
    x_pad = jnp.concatenate([xt, jnp.zeros((1, D), xt.dtype)], axis=0)

    def body(acc, blk):
        tok, wt, e = blk
        xb = x_pad[tok]
        hb = jax.nn.silu(xb @ w_gate[e]) * (xb @ w_up[e])
        yb = (hb @ w_down[e]).astype(f32) * wt[:, None]
        return acc.at[tok].add(yb), None

    acc, _ = lax.scan(body, jnp.zeros((N + 1, D), f32),
                      (buf_tok.reshape(n_blocks, MOE_BLOCK), buf_w.reshape(n_blocks, MOE_BLOCK), block_e))
    return acc[:N].reshape(B, T, D).astype(x.dtype)


def encoder_layer(x, norm_mix, w_in, ml_b_i, ml_b_f, ml_norm_g, rw_mu, rw_w0, rw_w2, rw_a0, rw_a2, rw_g2,
                  rw_k_k, rw_k_a, rw_r_k, rw_gn_g, rw_gn_b, w_up_a, w_up_b, w_out, norm_ffn,
                  w_router_group, b_router_group, w_router_expert, b_router_expert,
                  w_expert_gate, w_expert_up, w_expert_down):
    h = rms_norm(x, norm_mix)
    proj = h @ w_in
    ml_cols, rw_cols, gate_cols = split_cols(proj, (ML_COLS, RW_COLS, GATE_COLS))
    y_a = mlstm_branch(ml_cols, ml_b_i, ml_b_f, ml_norm_g) @ w_up_a
    y_b = rwkv7_branch(rw_cols, rw_mu, rw_w0, rw_w2, rw_a0, rw_a2, rw_g2, rw_k_k, rw_k_a, rw_r_k, rw_gn_g, rw_gn_b) @ w_up_b
    g_a, g_b = split_cols(gate_cols, (D_MODEL, D_MODEL))
    merged = jax.nn.sigmoid(g_a) * y_a + jax.nn.sigmoid(g_b) * y_b
    x = x + merged @ w_out
    x = x + hier_moe(rms_norm(x, norm_ffn), w_router_group, b_router_group, w_router_expert, b_router_expert,
                     w_expert_gate, w_expert_up, w_expert_down)
    return x


def encoder_forward(x, layer_weights, norm_final):
    for l in range(DEPTH):
        x = encoder_layer(x, *[w[l] for w in layer_weights])
    return rms_norm(x, norm_final)


def setup_inputs(seed: int = 0) -> dict:
    key = jax.random.key(seed)
    ks = jax.random.split(key, 30)
    f32 = jnp.float32

    def nrm(k, shape, scale):
        return jax.random.normal(k, shape, f32) * scale

    def unif(k, shape, lo, hi):
        return jax.random.uniform(k, shape, f32, minval=lo, maxval=hi)

    L = DEPTH
    return {
        'x_prompt': nrm(ks[0], (BATCH, SEQ, D_MODEL), 1.0),
        'x_sample': nrm(ks[1], (DEC_BATCH, DEC_SEQ, D_MODEL), 1.0),
        'norm_mix': 1.0 + nrm(ks[2], (L, D_MODEL), 0.02),
        'w_in': nrm(ks[3], (L, D_MODEL, D_IN), D_MODEL ** -0.5),
        'ml_b_i': nrm(ks[4], (L, 2, ML_HEADS), 0.1),
        'ml_b_f': unif(ks[5], (L, 2, ML_HEADS), 3.0, 6.0),
        'ml_norm_g': 1.0 + nrm(ks[6], (L, ML_V_W), 0.02),
        'rw_mu': unif(ks[7], (L, RW_COLS), 0.1, 0.9),
        'rw_w0': unif(ks[8], (L, 2, RW_W), -5.0, 0.0),
        'rw_w2': nrm(ks[9], (L, 2, RW_DECAY_LORA, RW_W), 0.5 * RW_DECAY_LORA ** -0.5),
        'rw_a0': nrm(ks[10], (L, RW_W), 0.1),
        'rw_a2': nrm(ks[11], (L, RW_AAA_LORA, RW_W), RW_AAA_LORA ** -0.5),
        'rw_g2': nrm(ks[12], (L, RW_GATE_LORA, RW_W), RW_GATE_LORA ** -0.5),
        'rw_k_k': 0.85 + nrm(ks[13], (L, RW_W), 0.05),
        'rw_k_a': 1.0 + nrm(ks[14], (L, RW_W), 0.05),
        'rw_r_k': nrm(ks[15], (L, RW_HEADS, RW_HEAD), 0.1),
        'rw_gn_g': 1.0 + nrm(ks[16], (L, RW_W), 0.02),
        'rw_gn_b': nrm(ks[17], (L, RW_W), 0.02),
        'w_up_a': nrm(ks[18], (L, ML_V_W, D_MODEL), ML_V_W ** -0.5),
        'w_up_b': nrm(ks[19], (L, RW_W, D_MODEL), RW_W ** -0.5),
        'w_out': nrm(ks[20], (L, D_MODEL, D_MODEL), D_MODEL ** -0.5),
        'norm_ffn': 1.0 + nrm(ks[21], (L, D_MODEL), 0.02),
        'w_router_group': nrm(ks[22], (L, D_MODEL, N_GROUPS), D_MODEL ** -0.5),
        'b_router_group': nrm(ks[23], (L, N_GROUPS), 0.01),
        'w_router_expert': nrm(ks[24], (L, D_MODEL, N_EXPERTS), D_MODEL ** -0.5),
        'b_router_expert': nrm(ks[25], (L, N_EXPERTS), 0.01),
        'w_expert_gate': nrm(ks[26], (L, N_EXPERTS, D_MODEL, D_EXPERT), D_MODEL ** -0.5),
        'w_expert_up': nrm(ks[27], (L, N_EXPERTS, D_MODEL, D_EXPERT), D_MODEL ** -0.5),
        'w_expert_down': nrm(ks[28], (L, N_EXPERTS, D_EXPERT, D_MODEL), D_EXPERT ** -0.5),
        'norm_final': 1.0 + nrm(ks[29], (D_MODEL,), 0.02),
    }


def reference(x_prompt, x_sample, norm_mix, w_in, ml_b_i, ml_b_f, ml_norm_g, rw_mu, rw_w0, rw_w2, rw_a0, rw_a2,
              rw_g2, rw_k_k, rw_k_a, rw_r_k, rw_gn_g, rw_gn_b, w_up_a, w_up_b, w_out, norm_ffn,
              w_router_group, b_router_group, w_router_expert, b_router_expert,
              w_expert_gate, w_expert_up, w_expert_down, norm_final):
    layer_weights = (norm_mix, w_in, ml_b_i, ml_b_f, ml_norm_g, rw_mu, rw_w0, rw_w2, rw_a0, rw_a2, rw_g2,
                     rw_k_k, rw_k_a, rw_r_k, rw_gn_g, rw_gn_b, w_up_a, w_up_b, w_out, norm_ffn,
                     w_router_group, b_router_group, w_router_expert, b_router_expert,
                     w_expert_gate, w_expert_up, w_expert_down)
    y_prompt = encoder_forward(x_prompt, layer_weights, norm_final)
    y_sample = encoder_forward(x_sample, layer_weights, norm_final)
    return (y_prompt, y_sample)
```

```python
import functools

import jax
import jax.numpy as jnp
from jax import lax
from jax.experimental import pallas as pl
from jax.experimental.pallas import tpu as pltpu

f32 = jnp.float32
bf16 = jnp.bfloat16
i32 = jnp.int32

D_MODEL = 1024
ML_HEADS = 4
ML_DQK = 128
ML_DV = 256
ML_QK_W = ML_HEADS * ML_DQK
ML_V_W = ML_HEADS * ML_DV
ML_COLS = 2 * ML_QK_W + 2 * ML_V_W + 4 * ML_HEADS
RW_HEAD = 64
RW_HEADS = 16
RW_W = 1024
RW_PAIRS = RW_HEADS // 2
RW_COLS = 3 * RW_W + 64 + 64 + 64 + 128
N_GROUPS = 8
EXPERTS_PER_GROUP = 8
N_EXPERTS = 64
D_EXPERT = 512
NORM_EPS = 1e-6
RW_GN_EPS = 64e-5
L2_EPS = 1e-12

LANES = 128
SUBLANES = 8
W_ALL = 8 * D_MODEL + 512
SMALL_BLK = (8 * D_MODEL) // 512
ML_CHUNK = 128
RW_CHUNK = 64
MOE_BLOCK = 256
VMEM_LIMIT = 56 * 1024 * 1024


def _cparams(sem):
    return pltpu.CompilerParams(dimension_semantics=sem, vmem_limit_bytes=VMEM_LIMIT)


def _dot(a, b):
    return jnp.dot(a.astype(bf16), b.astype(bf16), preferred_element_type=f32)


def _dot_nt(a, b):
    return lax.dot_general(a.astype(bf16), b.astype(bf16), (((1,), (1,)), ((), ())),
                           preferred_element_type=f32)


def _dot_tn(a, b):
    return lax.dot_general(a.astype(bf16), b.astype(bf16), (((0,), (0,)), ((), ())),
                           preferred_element_type=f32)


def _sigmoid(x):
    return 1.0 / (1.0 + jnp.exp(-x))


def _softplus(x):
    return jnp.maximum(x, 0.0) + jnp.log1p(jnp.exp(-jnp.abs(x)))


def _proj_body(x_ref, g_ref, w_ref, o_ref, xn_ref):
    @pl.when(pl.program_id(1) == 0)
    def _():
        x = x_ref[...]
        ms = jnp.mean(x * x, axis=-1, keepdims=True)
        xn_ref[...] = (x * lax.rsqrt(ms + NORM_EPS) * g_ref[...]).astype(bf16)

    o_ref[...] = jnp.dot(xn_ref[...], w_ref[...], preferred_element_type=f32)


def _proj(x2d, g, w_all):
    n = x2d.shape[0]
    tm = min(1024, n)
    tn = 512
    return pl.pallas_call(
        _proj_body,
        grid=(n // tm, W_ALL // tn),
        in_specs=[pl.BlockSpec((tm, D_MODEL), lambda i, j: (i, 0)),
                  pl.BlockSpec((1, D_MODEL), lambda i, j: (0, 0)),
                  pl.BlockSpec((D_MODEL, tn), lambda i, j: (0, j))],
        out_specs=pl.BlockSpec((tm, tn), lambda i, j: (i, j)),
        out_shape=jax.ShapeDtypeStruct((n, W_ALL), f32),
        scratch_shapes=[pltpu.VMEM((tm, D_MODEL), bf16)],
        compiler_params=_cparams(("parallel", "arbitrary")),
        name="proj",
    )(x2d, g, w_all)


def _log_sigmoid(x):
    return jnp.minimum(x, 0.0) - jnp.log1p(jnp.exp(-jnp.abs(x)))


def _mlstm_body(reverse, *refs):
    if reverse:
        (bi_ref, bf_ref, q_ref, k_ref, v_ref, g_ref, gt_ref, hf_ref, o_ref, ng_ref,
         out_ref, c_ref, n_ref, m_ref) = refs
    else:
        bi_ref, bf_ref, q_ref, k_ref, v_ref, g_ref, gt_ref, out_ref, c_ref, n_ref, m_ref = refs
    L = ML_CHUNK
    d = 1 if reverse else 0

    @pl.when(pl.program_id(1) == 0)
    def _():
        c_ref[...] = jnp.zeros_like(c_ref)
        n_ref[...] = jnp.zeros_like(n_ref)
        m_ref[...] = jnp.zeros_like(m_ref)

    rowi = lax.broadcasted_iota(i32, (L, L), 0)
    coli = lax.broadcasted_iota(i32, (L, L), 1)
    mask = (coli >= rowi) if reverse else (coli <= rowi)
    mask_t = (rowi >= coli) if reverse else (rowi <= coli)
    gates = g_ref[0]
    gt = gt_ref[0]
    scale = ML_DQK ** -0.5
    for h in range(ML_HEADS):
        ii = 4 * d + h
        fi = 8 + 4 * d + h
        b_i = bi_ref[d, h]
        b_f = bf_ref[d, h]
        i_col = gates[:, ii:ii + 1] + b_i
        lf_col = _log_sigmoid(gates[:, fi:fi + 1] + b_f)
        i_row = gt[ii:ii + 1, :] + b_i
        lf_row = _log_sigmoid(gt[fi:fi + 1, :] + b_f)
        b_col = jnp.sum(jnp.where(mask, lf_row, 0.0), axis=1, keepdims=True)
        b_row = jnp.sum(jnp.where(mask_t, lf_col, 0.0), axis=0, keepdims=True)
        logw = jnp.where(mask, b_col - b_row + i_row, -jnp.inf)
        m_prev = m_ref[h]
        m_inter = b_col + m_prev
        m_t = jnp.maximum(m_inter, jnp.max(logw, axis=1, keepdims=True))
        qh = q_ref[0, :, h * ML_DQK:(h + 1) * ML_DQK] * scale
        kh = k_ref[0, :, h * ML_DQK:(h + 1) * ML_DQK]
        vh = v_ref[0, :, h * ML_DV:(h + 1) * ML_DV]
        s = _dot_nt(qh, kh) * jnp.exp(logw - m_t)
        inter = jnp.exp(m_inter - m_t)
        c_h = c_ref[h]
        n_h = n_ref[h]
        num = _dot(s, vh) + inter * _dot(qh, c_h)
        den = jnp.sum(s, axis=1, keepdims=True) + inter * jnp.sum(qh * n_h, axis=1, keepdims=True)
        hh = num / jnp.maximum(jnp.abs(den), jnp.exp(-m_t))
        b_last = b_col[0:1] if reverse else b_col[L - 1:L]
        logu_col = b_last - b_col + i_col
        logu_row = b_last - b_row + i_row
        m_new = jnp.maximum(b_last + m_prev, jnp.max(logu_row, axis=1, keepdims=True))
        decay = jnp.exp(b_last + m_prev - m_new)
        ku = kh * jnp.exp(logu_col - m_new)
        c_ref[h] = decay * c_h + _dot_tn(ku, vh)
        n_ref[h] = decay * n_h + jnp.sum(ku, axis=0, keepdims=True)
        m_ref[h] = m_new
        sl = slice(h * ML_DV, (h + 1) * ML_DV)
        if reverse:
            tot = hf_ref[0, :, sl] + hh
            tot = tot * lax.rsqrt(jnp.mean(tot * tot, axis=-1, keepdims=True) + NORM_EPS)
            out_ref[0, :, sl] = _sigmoid(o_ref[0, :, sl]) * (tot * ng_ref[:, sl])
        else:
            out_ref[0, :, sl] = hh


def _mlstm(proj3, gates_t, b_i, b_f, norm_g):
    B, T, _ = proj3.shape
    L = ML_CHUNK
    nc = T // L
    smem = pl.BlockSpec(memory_space=pltpu.SMEM)
    scratch = [pltpu.VMEM((ML_HEADS, ML_DQK, ML_DV), f32),
               pltpu.VMEM((ML_HEADS, 1, ML_DQK), f32),
               pltpu.VMEM((ML_HEADS, 1, 1), f32)]

    def specs(cm):
        return [smem, smem,
                pl.BlockSpec((1, L, ML_QK_W), lambda b, c: (b, cm(c), 0)),
                pl.BlockSpec((1, L, ML_QK_W), lambda b, c: (b, cm(c), 1)),
                pl.BlockSpec((1, L, ML_V_W), lambda b, c: (b, cm(c), 1)),
                pl.BlockSpec((1, L, LANES), lambda b, c: (b, cm(c), (8 * D_MODEL) // LANES)),
                pl.BlockSpec((1, 16, L), lambda b, c: (b, 0, cm(c)))]

    fwd = lambda c: c
    h_fwd = pl.pallas_call(
        functools.partial(_mlstm_body, False),
        grid=(B, nc),
        in_specs=specs(fwd),
        out_specs=pl.BlockSpec((1, L, ML_V_W), lambda b, c: (b, c, 0)),
        out_shape=jax.ShapeDtypeStruct((B, T, ML_V_W), f32),
        scratch_shapes=scratch,
        compiler_params=_cparams(("parallel", "arbitrary")),
        name="mlstm_fwd",
    )(b_i, b_f, proj3, proj3, proj3, proj3, gates_t)
    bwd = lambda c: nc - 1 - c
    return pl.pallas_call(
        functools.partial(_mlstm_body, True),
        grid=(B, nc),
        in_specs=specs(bwd) + [
            pl.BlockSpec((1, L, ML_V_W), lambda b, c: (b, bwd(c), 0)),
            pl.BlockSpec((1, L, ML_V_W), lambda b, c: (b, bwd(c), 2)),
            pl.BlockSpec((1, ML_V_W), lambda b, c: (0, 0))],
        out_specs=pl.BlockSpec((1, L, ML_V_W), lambda b, c: (b, bwd(c), 0)),
        out_shape=jax.ShapeDtypeStruct((B, T, ML_V_W), f32),
        scratch_shapes=scratch,
        compiler_params=_cparams(("parallel", "arbitrary")),
        name="mlstm_bwd",
    )(b_i, b_f, proj3, proj3, proj3, proj3, gates_t, h_fwd, proj3, norm_g)


def _shift(p, prev8, next8, mu, first, last):
    tt = p.shape[0]
    rows = lax.broadcasted_iota(i32, p.shape, 0)
    prev_row = jnp.where(first, 0.0, prev8[SUBLANES - 1:SUBLANES])
    next_row = jnp.where(last, 0.0, next8[0:1])
    pm = jnp.where(rows == 0, prev_row, pltpu.roll(p, 1, axis=0))
    nx = jnp.where(rows == tt - 1, next_row, pltpu.roll(p, tt - 1, axis=0))
    return p + mu * (0.5 * (pm + nx) - p)


def _rw_prep_body(r_ref, rp_ref, rn_ref, k_ref, kp_ref, kn_ref, v_ref, vp_ref, vn_ref,
                  l_ref, lp_ref, ln_ref, mur_ref, muk_ref, muv_ref, mul_ref,
                  w0_ref, w2_ref, a0_ref, a2_ref, g2_ref,
                  ro_ref, ko_ref, vo_ref, ao_ref, lwf_ref, lwb_ref, go_ref):
    i = pl.program_id(1)
    first = i == 0
    last = i == pl.num_programs(1) - 1
    r = _shift(r_ref[0], rp_ref[0], rn_ref[0], mur_ref[...], first, last)
    k = _shift(k_ref[0], kp_ref[0], kn_ref[0], muk_ref[...], first, last)
    v = _shift(v_ref[0], vp_ref[0], vn_ref[0], muv_ref[...], first, last)
    lo = _shift(l_ref[0], lp_ref[0], ln_ref[0], mul_ref[...], first, last)
    wd_f = lo[:, 128:192]
    wd_b = lo[:, 192:256]
    ad = lo[:, 256:320]
    gd = lo[:, 320:448]

    def log_decay(wd, d):
        wl = w0_ref[d:d + 1, :] + jnp.dot(jnp.tanh(wd), w2_ref[d], precision=lax.Precision.HIGHEST,
                                          preferred_element_type=f32)
        return -jnp.exp(-_softplus(-wl) - 0.5)

    lw_f = log_decay(wd_f, 0)
    lw_b = log_decay(wd_b, 1)
    a = _sigmoid(a0_ref[...] + _dot(ad, a2_ref[...]))
    go_ref[0] = _dot(_sigmoid(gd), g2_ref[...])
    for hp in range(RW_PAIRS):
        sl = slice(hp * LANES, (hp + 1) * LANES)
        ro_ref[0, hp] = r[:, sl]
        ko_ref[0, hp] = k[:, sl]
        vo_ref[0, hp] = v[:, sl]
        ao_ref[0, hp] = a[:, sl]
        lwf_ref[0, hp] = lw_f[:, sl]
        lwb_ref[0, hp] = lw_b[:, sl]


def _rw_prep(proj3, mu_r, mu_k, mu_v, mu_l, w0, w2, a0, a2, g2):
    B, T, _ = proj3.shape
    tt = min(256, T)
    nt = T // tt
    g8 = tt // SUBLANES
    n8 = T // SUBLANES

    def tile(width, blk):
        return [pl.BlockSpec((1, tt, width), lambda b, i: (b, i, blk)),
                pl.BlockSpec((1, SUBLANES, width), lambda b, i: (b, jnp.maximum(i * g8 - 1, 0), blk)),
                pl.BlockSpec((1, SUBLANES, width), lambda b, i: (b, jnp.minimum((i + 1) * g8, n8 - 1), blk))]

    full = lambda shape: pl.BlockSpec(shape, lambda b, i: (0,) * len(shape))
    in_specs = (tile(D_MODEL, 3) + tile(D_MODEL, 4) + tile(D_MODEL, 5) + tile(512, SMALL_BLK)
                + [full((1, RW_W))] * 3 + [full((1, 512)), full((2, RW_W)), full((2, 64, RW_W)),
                                           full((1, RW_W)), full((64, RW_W)), full((128, RW_W))])
    hm = jax.ShapeDtypeStruct((B, RW_PAIRS, T, LANES), f32)
    hm_spec = pl.BlockSpec((1, RW_PAIRS, tt, LANES), lambda b, i: (b, 0, i, 0))
    return pl.pallas_call(
        _rw_prep_body,
        grid=(B, nt),
        in_specs=in_specs,
        out_specs=[hm_spec] * 6 + [pl.BlockSpec((1, tt, RW_W), lambda b, i: (b, i, 0))],
        out_shape=[hm] * 6 + [jax.ShapeDtypeStruct((B, T, RW_W), f32)],
        compiler_params=_cparams(("parallel", "parallel")),
        name="rw_prep",
    )(*([proj3] * 12), mu_r, mu_k, mu_v, mu_l, w0, w2, a0, a2, g2)


def _chunk_cumsum(x, reverse):
    L = x.shape[0]
    rows = lax.broadcasted_iota(i32, x.shape, 0)
    s = 1
    while s < L:
        if reverse:
            x = x + jnp.where(rows < L - s, pltpu.roll(x, L - s, axis=0), 0.0)
        else:
            x = x + jnp.where(rows >= s, pltpu.roll(x, s, axis=0), 0.0)
        s *= 2
    return x


def _rw_unit(r, kp, v, kk, bh, lw, cum, S, reverse, rowi, coli):
    L = r.shape[0]
    tot = cum[0:1] if reverse else cum[L - 1:L]
    ge = jnp.exp(cum)
    gi = jnp.exp(-cum)
    gp = jnp.exp(cum - lw)
    gl = jnp.exp(tot - cum)
    lhs = jnp.concatenate([-kk * gp, r * ge], axis=0)
    rhs = jnp.concatenate([bh * gi, kp * gi], axis=0)
    G = _dot_nt(lhs, rhs)
    strict = (coli > rowi) if reverse else (coli < rowi)
    incl = (coli >= rowi) if reverse else (coli <= rowi)
    a_ab = jnp.where(strict, G[:L, :L], 0.0)
    a_ak = jnp.where(strict, G[:L, L:], 0.0)
    a_rb = jnp.where(incl, G[L:, :L], 0.0)
    a_rk = jnp.where(incl, G[L:, L:], 0.0)
    sv = _dot_nt(lhs, S)
    av = _dot(jnp.concatenate([a_ak, a_rk], axis=0), v)
    y = sv[:L] + av[:L]
    p = a_ab
    s = 1
    while True:
        y = y + _dot(p, y)
        s *= 2
        if s >= L:
            break
        p = _dot(p, p)
    out = sv[L:] + av[L:] + _dot(a_rb, y)
    s_new = S * jnp.exp(tot) + _dot_tn(jnp.concatenate([y, v], axis=0),
                                       jnp.concatenate([bh * gl, kp * gl], axis=0))
    return out, s_new


def _rw_scan_body(rf_ref, kf_ref, vf_ref, af_ref, lwf_ref, rb_ref, kb_ref, vb_ref, ab_ref, lwb_ref,
                  kk_ref, ka_ref, of_ref, ob_ref, s_ref):
    L = RW_CHUNK
    H = RW_HEAD

    @pl.when(pl.program_id(1) == 0)
    def _():
        s_ref[...] = jnp.zeros_like(s_ref)

    rowi = lax.broadcasted_iota(i32, (L, L), 0)
    coli = lax.broadcasted_iota(i32, (L, L), 1)

    def pair(hp, carry):
        k_k = kk_ref[hp]
        k_a = ka_ref[hp]
        for d, (r_ref, k_ref, v_ref, a_ref, lw_ref, o_ref) in enumerate(
                ((rf_ref, kf_ref, vf_ref, af_ref, lwf_ref, of_ref),
                 (rb_ref, kb_ref, vb_ref, ab_ref, lwb_ref, ob_ref))):
            r = r_ref[0, hp]
            k = k_ref[0, hp]
            v = v_ref[0, hp]
            a = a_ref[0, hp]
            lw = lw_ref[0, hp]
            cum = _chunk_cumsum(lw, d == 1)
            kkr = k * k_k
            kp = k * (1.0 + (a - 1.0) * k_a)
            outs = []
            for j in range(2):
                sl = slice(j * H, (j + 1) * H)
                kkj = kkr[:, sl]
                nrm = jnp.sqrt(jnp.sum(kkj * kkj, axis=-1, keepdims=True))
                kkj = kkj / jnp.maximum(nrm, L2_EPS)
                o, s_new = _rw_unit(r[:, sl], kp[:, sl], v[:, sl], kkj, kkj * a[:, sl], lw[:, sl],
                                    cum[:, sl], s_ref[d, 2 * hp + j], d == 1, rowi, coli)
                s_ref[d, 2 * hp + j] = s_new
                outs.append(o)
            o_ref[0, hp] = jnp.concatenate(outs, axis=1)
        return carry

    lax.fori_loop(0, RW_PAIRS, pair, 0)


def _rw_scan(r, k, v, a, lwf, lwb, k_k, k_a):
    B, _, T, _ = r.shape
    L = RW_CHUNK
    nc = T // L
    fs = pl.BlockSpec((1, RW_PAIRS, L, LANES), lambda b, c: (b, 0, c, 0))
    bs = pl.BlockSpec((1, RW_PAIRS, L, LANES), lambda b, c: (b, 0, nc - 1 - c, 0))
    ps = pl.BlockSpec((RW_PAIRS, 1, LANES), lambda b, c: (0, 0, 0))
    hm = jax.ShapeDtypeStruct((B, RW_PAIRS, T, LANES), f32)
    return pl.pallas_call(
        _rw_scan_body,
        grid=(B, nc),
        in_specs=[fs] * 5 + [bs] * 5 + [ps, ps],
        out_specs=[fs, bs],
        out_shape=[hm, hm],
        scratch_shapes=[pltpu.VMEM((2, RW_HEADS, RW_HEAD, RW_HEAD), f32)],
        compiler_params=_cparams(("parallel", "arbitrary")),
        name="rw_scan",
    )(r, k, v, a, lwf, r, k, v, a, lwb, k_k, k_a)


def _rw_post_body(of_ref, ob_ref, r_ref, k_ref, v_ref, a_ref, g_ref, ka_ref, rk_ref, gg_ref, gb_ref, o_ref):
    H = RW_HEAD
    for hp in range(RW_PAIRS):
        out = of_ref[0, hp] + ob_ref[0, hp]
        r = r_ref[0, hp]
        v = v_ref[0, hp]
        kp = k_ref[0, hp] * (1.0 + (a_ref[0, hp] - 1.0) * ka_ref[hp])
        rk = r * kp * rk_ref[hp]
        halves = []
        for j in range(2):
            sl = slice(j * H, (j + 1) * H)
            oj = out[:, sl]
            mean = jnp.mean(oj, axis=-1, keepdims=True)
            cj = oj - mean
            var = jnp.mean(cj * cj, axis=-1, keepdims=True)
            nj = cj * lax.rsqrt(var + RW_GN_EPS)
            bonus = jnp.sum(rk[:, sl], axis=-1, keepdims=True) * v[:, sl]
            halves.append((nj, bonus))
        normed = jnp.concatenate([halves[0][0], halves[1][0]], axis=1)
        bonus = jnp.concatenate([halves[0][1], halves[1][1]], axis=1)
        lanes = slice(hp * LANES, (hp + 1) * LANES)
        o_ref[0, :, lanes] = (normed * gg_ref[hp] + gb_ref[hp] + bonus) * g_ref[0, :, lanes]


def _rw_post(out_f, out_b, r, k, v, a, g, k_a, r_k, gn_g, gn_b):
    B, _, T, _ = r.shape
    tt = min(256, T)
    hs = pl.BlockSpec((1, RW_PAIRS, tt, LANES), lambda b, i: (b, 0, i, 0))
    ps = pl.BlockSpec((RW_PAIRS, 1, LANES), lambda b, i: (0, 0, 0))
    ts = pl.BlockSpec((1, tt, RW_W), lambda b, i: (b, i, 0))
    return pl.pallas_call(
        _rw_post_body,
        grid=(B, T // tt),
        in_specs=[hs] * 6 + [ts] + [ps] * 4,
        out_specs=ts,
        out_shape=jax.ShapeDtypeStruct((B, T, RW_W), f32),
        compiler_params=_cparams(("parallel", "parallel")),
        name="rw_post",
    )(out_f, out_b, r, k, v, a, g, k_a, r_k, gn_g, gn_b)


def _merge_body(ml_ref, rw_ref, ga_ref, gb_ref, x_ref, wa_ref, wb_ref, wo_ref, nf_ref, wr_ref, br_ref,
                x1_ref, xn_ref, lg_ref):
    y_a = jnp.dot(ml_ref[...].astype(bf16), wa_ref[...], preferred_element_type=f32)
    y_b = jnp.dot(rw_ref[...].astype(bf16), wb_ref[...], preferred_element_type=f32)
    merged = _sigmoid(ga_ref[...]) * y_a + _sigmoid(gb_ref[...]) * y_b
    x1 = x_ref[...] + jnp.dot(merged.astype(bf16), wo_ref[...], preferred_element_type=f32)
    x1_ref[...] = x1
    xn = x1 * lax.rsqrt(jnp.mean(x1 * x1, axis=-1, keepdims=True) + NORM_EPS) * nf_ref[...]
    xn_ref[...] = xn
    lg_ref[...] = jnp.dot(xn, wr_ref[...], precision=lax.Precision.HIGHEST,
                          preferred_element_type=f32) + br_ref[...]


def _merge(ml_out, rw_out, proj, x2d, w_up_a, w_up_b, w_out, norm_ffn, w_router, b_router):
    n = x2d.shape[0]
    tm = min(512, n)
    row = lambda blk: pl.BlockSpec((tm, D_MODEL), lambda i: (i, blk))
    full = lambda shape: pl.BlockSpec(shape, lambda i: (0,) * len(shape))
    return pl.pallas_call(
        _merge_body,
        grid=(n // tm,),
        in_specs=[row(0), row(0), row(6), row(7), row(0),
                  full((D_MODEL, D_MODEL)), full((D_MODEL, D_MODEL)), full((D_MODEL, D_MODEL)),
                  full((1, D_MODEL)), full((D_MODEL, LANES)), full((1, LANES))],
        out_specs=[row(0), row(0), pl.BlockSpec((tm, LANES), lambda i: (i, 0))],
        out_shape=[jax.ShapeDtypeStruct((n, D_MODEL), f32), jax.ShapeDtypeStruct((n, D_MODEL), f32),
                   jax.ShapeDtypeStruct((n, LANES), f32)],
        compiler_params=_cparams(("parallel",)),
        name="merge",
    )(ml_out, rw_out, proj, proj, x2d, w_up_a, w_up_b, w_out, norm_ffn, w_router, b_router)


def _first_argmax(p, width):
    lane = lax.broadcasted_iota(i32, p.shape, 1)
    top = jnp.max(p, axis=-1, keepdims=True)
    idx = jnp.min(jnp.where(p == top, lane, width), axis=-1, keepdims=True)
    return top, idx, lane


def _route_body(lg_ref, meta_ref, gate_ref, cnt_ref, carry_ref):
    @pl.when(pl.program_id(0) == 0)
    def _():
        carry_ref[...] = jnp.zeros_like(carry_ref)

    lg = lg_ref[...]
    tr = lg.shape[0]
    gl = lg[:, 0:N_GROUPS]
    ge = jnp.exp(gl - jnp.max(gl, axis=-1, keepdims=True))
    p_group = ge / jnp.sum(ge, axis=-1, keepdims=True)
    p_g, g_idx, _ = _first_argmax(p_group, N_GROUPS)
    el = jnp.zeros((tr, EXPERTS_PER_GROUP), f32)
    for g in range(N_GROUPS):
        lo = N_GROUPS + g * EXPERTS_PER_GROUP
        el = el + jnp.where(g_idx == g, 1.0, 0.0) * lg[:, lo:lo + EXPERTS_PER_GROUP]
    ee = jnp.exp(el - jnp.max(el, axis=-1, keepdims=True))
    p_exp = ee / jnp.sum(ee, axis=-1, keepdims=True)
    p1, i1, lane8 = _first_argmax(p_exp, EXPERTS_PER_GROUP)
    p2, i2, _ = _first_argmax(jnp.where(lane8 == i1, -1.0, p_exp), EXPERTS_PER_GROUP)
    psum = p1 + p2
    g1 = p_g * p1 / psum
    g2 = p_g * p2 / psum
    e0 = g_idx * EXPERTS_PER_GROUP + i1
    e1 = g_idx * EXPERTS_PER_GROUP + i2
    lane = lax.broadcasted_iota(i32, (tr, LANES), 1)
    hit0 = lane == e0
    hit1 = lane == e1
    onehot = jnp.where(hit0 | hit1, 1.0, 0.0)
    rowi = lax.broadcasted_iota(i32, (tr, tr), 0)
    coli = lax.broadcasted_iota(i32, (tr, tr), 1)
    before = jnp.where(coli < rowi, 1.0, 0.0)
    prefix = _dot(before, onehot) + carry_ref[...]
    r0 = jnp.sum(jnp.where(hit0, prefix, 0.0), axis=-1, keepdims=True).astype(i32)
    r1 = jnp.sum(jnp.where(hit1, prefix, 0.0), axis=-1, keepdims=True).astype(i32)
    carry = carry_ref[...] + jnp.sum(onehot, axis=0, keepdims=True)
    carry_ref[...] = carry
    cnt_ref[...] = carry.astype(i32)
    meta_ref[...] = jnp.where(lane == 0, e0, jnp.where(lane == 1, e1, jnp.where(lane == 2, r0, r1)))
    gate_ref[...] = jnp.where(lane == 0, g1, g2)


def _route(logits):
    n = logits.shape[0]
    tr = min(512, n)
    return pl.pallas_call(
        _route_body,
        grid=(n // tr,),
        in_specs=[pl.BlockSpec((tr, LANES), lambda i: (i, 0))],
        out_specs=[pl.BlockSpec((tr, LANES), lambda i: (i, 0)), pl.BlockSpec((tr, LANES), lambda i: (i, 0)),
                   pl.BlockSpec((1, LANES), lambda i: (0, 0))],
        out_shape=[jax.ShapeDtypeStruct((n, LANES), i32), jax.ShapeDtypeStruct((n, LANES), f32),
                   jax.ShapeDtypeStruct((1, LANES), i32)],
        scratch_shapes=[pltpu.VMEM((1, LANES), f32)],
        compiler_params=_cparams(("arbitrary",)),
        name="route",
    )(logits)


def _scatter_body(pstart_ref, e0_ref, e1_ref, r0_ref, r1_ref, x_hbm, buf_in, xs_hbm, sem):
    del buf_in
    ts = e0_ref.shape[0]
    base = pl.program_id(0) * ts

    def row_copy(tok, dst):
        return pltpu.make_async_copy(x_hbm.at[pl.ds(tok, 1)], xs_hbm.at[pl.ds(dst, 1)], sem)

    def issue(t, c):
        row_copy(base + t, pstart_ref[e0_ref[t]] + r0_ref[t]).start()
        row_copy(base + t, pstart_ref[e1_ref[t]] + r1_ref[t]).start()
        return c

    lax.fori_loop(0, ts, issue, 0)

    def drain(t, c):
        row_copy(0, 0).wait()
        row_copy(0, 0).wait()
        return c

    lax.fori_loop(0, ts, drain, 0)


def _scatter(xn, pstart, e0, e1, r0, r1, n_rows):
    n = xn.shape[0]
    ts = min(512, n)
    tok = pl.BlockSpec((ts,), lambda i, ps: (i,), memory_space=pltpu.SMEM)
    anyspec = pl.BlockSpec(memory_space=pl.ANY)
    buf = jnp.zeros((n_rows, D_MODEL), f32)
    return pl.pallas_call(
        _scatter_body,
        grid_spec=pltpu.PrefetchScalarGridSpec(
            num_scalar_prefetch=1, grid=(n // ts,),
            in_specs=[tok, tok, tok, tok, anyspec, anyspec],
            out_specs=anyspec,
            scratch_shapes=[pltpu.SemaphoreType.DMA(())]),
        out_shape=jax.ShapeDtypeStruct((n_rows, D_MODEL), f32),
        input_output_aliases={6: 0},
        compiler_params=_cparams(("arbitrary",)),
        name="moe_scatter",
    )(pstart, e0, e1, r0, r1, xn, buf)


def _experts_body(be_ref, nu_ref, xs_ref, wg_ref, wu_ref, wd_ref, ys_ref):
    del be_ref

    @pl.when(pl.program_id(0) < nu_ref[0])
    def _():
        xb = xs_ref[...].astype(bf16)
        hg = jnp.dot(xb, wg_ref[0], preferred_element_type=f32)
        hu = jnp.dot(xb, wu_ref[0], preferred_element_type=f32)
        hb = hg * _sigmoid(hg) * hu
        ys_ref[...] = jnp.dot(hb.astype(bf16), wd_ref[0], preferred_element_type=f32)

    @pl.when(pl.program_id(0) >= nu_ref[0])
    def _():
        ys_ref[...] = jnp.zeros_like(ys_ref)


def _experts(xs, block_e, n_used, w_gate, w_up, w_down):
    n_rows = xs.shape[0]
    nb = n_rows // MOE_BLOCK
    return pl.pallas_call(
        _experts_body,
        grid_spec=pltpu.PrefetchScalarGridSpec(
            num_scalar_prefetch=2, grid=(nb,),
            in_specs=[pl.BlockSpec((MOE_BLOCK, D_MODEL), lambda i, be, nu: (i, 0)),
                      pl.BlockSpec((1, D_MODEL, D_EXPERT), lambda i, be, nu: (be[i], 0, 0)),
                      pl.BlockSpec((1, D_MODEL, D_EXPERT), lambda i, be, nu: (be[i], 0, 0)),
                      pl.BlockSpec((1, D_EXPERT, D_MODEL), lambda i, be, nu: (be[i], 0, 0))],
            out_specs=pl.BlockSpec((MOE_BLOCK, D_MODEL), lambda i, be, nu: (i, 0))),
        out_shape=jax.ShapeDtypeStruct((n_rows, D_MODEL), f32),
        compiler_params=_cparams(("arbitrary",)),
        name="moe_experts",
    )(block_e, n_used, xs, w_gate, w_up, w_down)


def _combine_body(pstart_ref, e0_ref, e1_ref, r0_ref, r1_ref, x1_ref, gate_ref, nf_ref, ys_hbm,
                  o_ref, y0_ref, y1_ref, sem):
    tc = e0_ref.shape[0]

    def row_copy(src, dst_ref, t):
        return pltpu.make_async_copy(ys_hbm.at[pl.ds(src, 1)], dst_ref.at[pl.ds(t, 1)], sem)

    def issue(t, c):
        row_copy(pstart_ref[e0_ref[t]] + r0_ref[t], y0_ref, t).start()
        row_copy(pstart_ref[e1_ref[t]] + r1_ref[t], y1_ref, t).start()
        return c

    lax.fori_loop(0, tc, issue, 0)

    def drain(t, c):
        row_copy(0, y0_ref, 0).wait()
        row_copy(0, y1_ref, 0).wait()
        return c

    lax.fori_loop(0, tc, drain, 0)
    gates = gate_ref[...]
    x2 = x1_ref[...] + gates[:, 0:1] * y0_ref[...] + gates[:, 1:2] * y1_ref[...]
    o_ref[...] = x2 * lax.rsqrt(jnp.mean(x2 * x2, axis=-1, keepdims=True) + NORM_EPS) * nf_ref[...]


def _combine(x1, gates, norm_final, ys, pstart, e0, e1, r0, r1):
    n = x1.shape[0]
    tc = min(256, n)
    tok = pl.BlockSpec((tc,), lambda i, ps: (i,), memory_space=pltpu.SMEM)
    return pl.pallas_call(
        _combine_body,
        grid_spec=pltpu.PrefetchScalarGridSpec(
            num_scalar_prefetch=1, grid=(n // tc,),
            in_specs=[tok, tok, tok, tok,
                      pl.BlockSpec((tc, D_MODEL), lambda i, ps: (i, 0)),
                      pl.BlockSpec((tc, LANES), lambda i, ps: (i, 0)),
                      pl.BlockSpec((1, D_MODEL), lambda i, ps: (0, 0)),
                      pl.BlockSpec(memory_space=pl.ANY)],
            out_specs=pl.BlockSpec((tc, D_MODEL), lambda i, ps: (i, 0)),
            scratch_shapes=[pltpu.VMEM((tc, D_MODEL), f32), pltpu.VMEM((tc, D_MODEL), f32),
                            pltpu.SemaphoreType.DMA(())]),
        out_shape=jax.ShapeDtypeStruct((n, D_MODEL), f32),
        compiler_params=_cparams(("arbitrary",)),
        name="moe_combine",
    )(pstart, e0, e1, r0, r1, x1, gates, norm_final, ys)


def _prep_weights(w_in, rw_mu, w_router_group, b_router_group, w_router_expert, b_router_expert):
    rw0 = ML_COLS
    g0 = ML_COLS + RW_COLS
    zeros = lambda c: jnp.zeros((D_MODEL, c), w_in.dtype)
    w_all = jnp.concatenate([
        w_in[:, 0:3072],
        w_in[:, rw0:rw0 + 3072],
        w_in[:, g0:g0 + 2048],
        w_in[:, 3072:3088], zeros(112),
        w_in[:, rw0 + 3072:rw0 + 3392], zeros(64)], axis=1).astype(bf16)
    mu_r = rw_mu[None, 0:1024]
    mu_k = rw_mu[None, 1024:2048]
    mu_v = rw_mu[None, 2048:3072]
    mu_l = jnp.concatenate([jnp.zeros((128,), f32), rw_mu[3072:3392], jnp.zeros((64,), f32)])[None]
    w_router = jnp.concatenate([w_router_group, w_router_expert,
                                jnp.zeros((D_MODEL, LANES - N_GROUPS - N_EXPERTS), f32)], axis=1)
    b_router = jnp.concatenate([b_router_group, b_router_expert,
                                jnp.zeros((LANES - N_GROUPS - N_EXPERTS,), f32)])[None]
    return w_all, mu_r, mu_k, mu_v, mu_l, w_router, b_router


def _pairs(p):
    return p.reshape(RW_PAIRS, 1, LANES)


def _forward(x, norm_mix, w_all, mus, ml_b_i, ml_b_f, ml_norm_g, rw_w0, rw_w2, rw_a0, rw_a2, rw_g2,
             rw_k_k, rw_k_a, rw_r_k, rw_gn_g, rw_gn_b, w_up_a, w_up_b, w_out, norm_ffn,
             w_router, b_router, w_gate, w_up, w_down, norm_final):
    B, T, _ = x.shape
    n = B * T
    x2d = x.reshape(n, D_MODEL)
    proj = _proj(x2d, norm_mix[None], w_all)
    proj3 = proj.reshape(B, T, W_ALL)
    gates_t = jnp.swapaxes(proj3[:, :, 8 * D_MODEL:8 * D_MODEL + 16], 1, 2)
    ml_out = _mlstm(proj3, gates_t, ml_b_i, ml_b_f, ml_norm_g[None])
    mu_r, mu_k, mu_v, mu_l = mus
    r, k, v, a, lwf, lwb, g = _rw_prep(proj3, mu_r, mu_k, mu_v, mu_l, rw_w0, rw_w2, rw_a0[None], rw_a2, rw_g2)
    out_f, out_b = _rw_scan(r, k, v, a, lwf, lwb, _pairs(rw_k_k), _pairs(rw_k_a))
    rw_out = _rw_post(out_f, out_b, r, k, v, a, g, _pairs(rw_k_a), _pairs(rw_r_k), _pairs(rw_gn_g),
                      _pairs(rw_gn_b))
    x1, xn, logits = _merge(ml_out.reshape(n, ML_V_W), rw_out.reshape(n, RW_W), proj, x2d,
                            w_up_a, w_up_b, w_out, norm_ffn[None], w_router, b_router)
    meta, gates, counts = _route(logits)
    counts = counts[0, :N_EXPERTS]
    padded = ((counts + MOE_BLOCK - 1) // MOE_BLOCK) * MOE_BLOCK
    pend = jnp.cumsum(padded)
    pstart = (pend - padded).astype(i32)
    nb = (2 * n) // MOE_BLOCK + N_EXPERTS
    block_start = jnp.arange(nb, dtype=i32) * MOE_BLOCK
    block_e = jnp.minimum(jnp.searchsorted(pend, block_start, side='right'), N_EXPERTS - 1).astype(i32)
    n_used = (pend[-1:] // MOE_BLOCK).astype(i32)
    e0, e1, r0, r1 = meta[:, 0], meta[:, 1], meta[:, 2], meta[:, 3]
    xs = _scatter(xn, pstart, e0, e1, r0, r1, nb * MOE_BLOCK)
    ys = _experts(xs, block_e, n_used, w_gate, w_up, w_down)
    y = _combine(x1, gates, norm_final[None], ys, pstart, e0, e1, r0, r1)
    return y.reshape(B, T, D_MODEL)


def kernel(x_prompt, x_sample, norm_mix, w_in, ml_b_i, ml_b_f, ml_norm_g, rw_mu, rw_w0, rw_w2, rw_a0, rw_a2,
           rw_g2, rw_k_k, rw_k_a, rw_r_k, rw_gn_g, rw_gn_b, w_up_a, w_up_b, w_out, norm_ffn,
           w_router_group, b_router_group, w_router_expert, b_router_expert,
           w_expert_gate, w_expert_up, w_expert_down, norm_final):
    w_all, mu_r, mu_k, mu_v, mu_l, w_router, b_router = _prep_weights(
        w_in[0], rw_mu[0], w_router_group[0], b_router_group[0], w_router_expert[0], b_router_expert[0])
    args = (norm_mix[0], w_all, (mu_r, mu_k, mu_v, mu_l), ml_b_i[0], ml_b_f[0], ml_norm_g[0],
            rw_w0[0], rw_w2[0], rw_a0[0], rw_a2[0], rw_g2[0], rw_k_k[0], rw_k_a[0], rw_r_k[0].reshape(-1),
            rw_gn_g[0], rw_gn_b[0], w_up_a[0].astype(bf16), w_up_b[0].astype(bf16), w_out[0].astype(bf16),
            norm_ffn[0], w_router, b_router, w_expert_gate[0].astype(bf16), w_expert_up[0].astype(bf16),
            w_expert_down[0].astype(bf16), norm_final)
    return (_forward(x_prompt, *args), _forward(x_sample, *args))
```

```python
import functools

import jax
import jax.numpy as jnp
from jax import lax
from jax.experimental import pallas as pl
from jax.experimental.pallas import tpu as pltpu

f32 = jnp.float32
bf16 = jnp.bfloat16
i32 = jnp.int32

D_MODEL = 1024
ML_HEADS = 4
ML_DQK = 128
ML_DV = 256
ML_QK_W = ML_HEADS * ML_DQK
ML_V_W = ML_HEADS * ML_DV
ML_COLS = 2 * ML_QK_W + 2 * ML_V_W + 4 * ML_HEADS
RW_HEAD = 64
RW_HEADS = 16
RW_W = 1024
RW_PAIRS = RW_HEADS // 2
RW_COLS = 3 * RW_W + 64 + 64 + 64 + 128
N_GROUPS = 8
EXPERTS_PER_GROUP = 8
N_EXPERTS = 64
D_EXPERT = 512
NORM_EPS = 1e-6
RW_GN_EPS = 64e-5
L2_EPS = 1e-12

LANES = 128
SUBLANES = 8
W_ALL = 8 * D_MODEL + 512
SMALL_BLK = (8 * D_MODEL) // 512
ML_CHUNK = 128
RW_CHUNK = 64
RW_GROUP = 8
MOE_BLOCK = 256
VMEM_LIMIT = 56 * 1024 * 1024


def _cparams(sem):
    return pltpu.CompilerParams(dimension_semantics=sem, vmem_limit_bytes=VMEM_LIMIT)


def _dot(a, b):
    return jnp.dot(a.astype(bf16), b.astype(bf16), preferred_element_type=f32)


def _dot_nt(a, b):
    return lax.dot_general(a.astype(bf16), b.astype(bf16), (((1,), (1,)), ((), ())),
                           preferred_element_type=f32)


def _dot_tn(a, b):
    return lax.dot_general(a.astype(bf16), b.astype(bf16), (((0,), (0,)), ((), ())),
                           preferred_element_type=f32)


def _sigmoid(x):
    return 1.0 / (1.0 + jnp.exp(-x))


def _softplus(x):
    return jnp.maximum(x, 0.0) + jnp.log1p(jnp.exp(-jnp.abs(x)))


def _proj_body(x_ref, g_ref, w_ref, o_ref, xn_ref):
    @pl.when(pl.program_id(1) == 0)
    def _():
        x = x_ref[...]
        ms = jnp.mean(x * x, axis=-1, keepdims=True)
        xn_ref[...] = (x * lax.rsqrt(ms + NORM_EPS) * g_ref[...]).astype(bf16)

    o_ref[...] = jnp.dot(xn_ref[...], w_ref[...], preferred_element_type=f32)


def _proj(x2d, g, w_all):
    n = x2d.shape[0]
    tm = min(1024, n)
    tn = 512
    return pl.pallas_call(
        _proj_body,
        grid=(n // tm, W_ALL // tn),
        in_specs=[pl.BlockSpec((tm, D_MODEL), lambda i, j: (i, 0)),
                  pl.BlockSpec((1, D_MODEL), lambda i, j: (0, 0)),
                  pl.BlockSpec((D_MODEL, tn), lambda i, j: (0, j))],
        out_specs=pl.BlockSpec((tm, tn), lambda i, j: (i, j)),
        out_shape=jax.ShapeDtypeStruct((n, W_ALL), f32),
        scratch_shapes=[pltpu.VMEM((tm, D_MODEL), bf16)],
        compiler_params=_cparams(("parallel", "arbitrary")),
        name="proj",
    )(x2d, g, w_all)


def _log_sigmoid(x):
    return jnp.minimum(x, 0.0) - jnp.log1p(jnp.exp(-jnp.abs(x)))


def _mlstm_body(reverse, *refs):
    if reverse:
        (bi_ref, bf_ref, q_ref, k_ref, v_ref, g_ref, gt_ref, hf_ref, o_ref, ng_ref,
         out_ref, c_ref, n_ref, m_ref) = refs
    else:
        bi_ref, bf_ref, q_ref, k_ref, v_ref, g_ref, gt_ref, out_ref, c_ref, n_ref, m_ref = refs
    L = ML_CHUNK
    d = 1 if reverse else 0

    @pl.when(pl.program_id(1) == 0)
    def _():
        c_ref[...] = jnp.zeros_like(c_ref)
        n_ref[...] = jnp.zeros_like(n_ref)
        m_ref[...] = jnp.zeros_like(m_ref)

    rowi = lax.broadcasted_iota(i32, (L, L), 0)
    coli = lax.broadcasted_iota(i32, (L, L), 1)
    mask = (coli >= rowi) if reverse else (coli <= rowi)
    mask_t = (rowi >= coli) if reverse else (rowi <= coli)
    gates = g_ref[0]
    gt = gt_ref[0]
    scale = ML_DQK ** -0.5
    for h in range(ML_HEADS):
        ii = 4 * d + h
        fi = 8 + 4 * d + h
        b_i = bi_ref[d, h]
        b_f = bf_ref[d, h]
        i_col = gates[:, ii:ii + 1] + b_i
        lf_col = _log_sigmoid(gates[:, fi:fi + 1] + b_f)
        i_row = gt[ii:ii + 1, :] + b_i
        lf_row = _log_sigmoid(gt[fi:fi + 1, :] + b_f)
        b_col = jnp.sum(jnp.where(mask, lf_row, 0.0), axis=1, keepdims=True)
        b_row = jnp.sum(jnp.where(mask_t, lf_col, 0.0), axis=0, keepdims=True)
        logw = jnp.where(mask, b_col - b_row + i_row, -jnp.inf)
        m_prev = m_ref[h]
        m_inter = b_col + m_prev
        m_t = jnp.maximum(m_inter, jnp.max(logw, axis=1, keepdims=True))
        qh = q_ref[0, :, h * ML_DQK:(h + 1) * ML_DQK] * scale
        kh = k_ref[0, :, h * ML_DQK:(h + 1) * ML_DQK]
        vh = v_ref[0, :, h * ML_DV:(h + 1) * ML_DV]
        s = _dot_nt(qh, kh) * jnp.exp(logw - m_t)
        inter = jnp.exp(m_inter - m_t)
        c_h = c_ref[h]
        n_h = n_ref[h]
        num = _dot(s, vh) + inter * _dot(qh, c_h)
        den = jnp.sum(s, axis=1, keepdims=True) + inter * jnp.sum(qh * n_h, axis=1, keepdims=True)
        hh = num / jnp.maximum(jnp.abs(den), jnp.exp(-m_t))
        b_last = b_col[0:1] if reverse else b_col[L - 1:L]
        logu_col = b_last - b_col + i_col
        logu_row = b_last - b_row + i_row
        m_new = jnp.maximum(b_last + m_prev, jnp.max(logu_row, axis=1, keepdims=True))
        decay = jnp.exp(b_last + m_prev - m_new)
        ku = kh * jnp.exp(logu_col - m_new)
        c_ref[h] = decay * c_h + _dot_tn(ku, vh)
        n_ref[h] = decay * n_h + jnp.sum(ku, axis=0, keepdims=True)
        m_ref[h] = m_new
        sl = slice(h * ML_DV, (h + 1) * ML_DV)
        if reverse:
            tot = hf_ref[0, :, sl] + hh
            tot = tot * lax.rsqrt(jnp.mean(tot * tot, axis=-1, keepdims=True) + NORM_EPS)
            out_ref[0, :, sl] = _sigmoid(o_ref[0, :, sl]) * (tot * ng_ref[:, sl])
        else:
            out_ref[0, :, sl] = hh


def _mlstm(proj3, gates_t, b_i, b_f, norm_g):
    B, T, _ = proj3.shape
    L = ML_CHUNK
    nc = T // L
    smem = pl.BlockSpec(memory_space=pltpu.SMEM)
    scratch = [pltpu.VMEM((ML_HEADS, ML_DQK, ML_DV), f32),
               pltpu.VMEM((ML_HEADS, 1, ML_DQK), f32),
               pltpu.VMEM((ML_HEADS, 1, 1), f32)]

    def specs(cm):
        return [smem, smem,
                pl.BlockSpec((1, L, ML_QK_W), lambda b, c: (b, cm(c), 0)),
                pl.BlockSpec((1, L, ML_QK_W), lambda b, c: (b, cm(c), 1)),
                pl.BlockSpec((1, L, ML_V_W), lambda b, c: (b, cm(c), 1)),
                pl.BlockSpec((1, L, LANES), lambda b, c: (b, cm(c), (8 * D_MODEL) // LANES)),
                pl.BlockSpec((1, 16, L), lambda b, c: (b, 0, cm(c)))]

    fwd = lambda c: c
    h_fwd = pl.pallas_call(
        functools.partial(_mlstm_body, False),
        grid=(B, nc),
        in_specs=specs(fwd),
        out_specs=pl.BlockSpec((1, L, ML_V_W), lambda b, c: (b, c, 0)),
        out_shape=jax.ShapeDtypeStruct((B, T, ML_V_W), f32),
        scratch_shapes=scratch,
        compiler_params=_cparams(("parallel", "arbitrary")),
        name="mlstm_fwd",
    )(b_i, b_f, proj3, proj3, proj3, proj3, gates_t)
    bwd = lambda c: nc - 1 - c
    return pl.pallas_call(
        functools.partial(_mlstm_body, True),
        grid=(B, nc),
        in_specs=specs(bwd) + [
            pl.BlockSpec((1, L, ML_V_W), lambda b, c: (b, bwd(c), 0)),
            pl.BlockSpec((1, L, ML_V_W), lambda b, c: (b, bwd(c), 2)),
            pl.BlockSpec((1, ML_V_W), lambda b, c: (0, 0))],
        out_specs=pl.BlockSpec((1, L, ML_V_W), lambda b, c: (b, bwd(c), 0)),
        out_shape=jax.ShapeDtypeStruct((B, T, ML_V_W), f32),
        scratch_shapes=scratch,
        compiler_params=_cparams(("parallel", "arbitrary")),
        name="mlstm_bwd",
    )(b_i, b_f, proj3, proj3, proj3, proj3, gates_t, h_fwd, proj3, norm_g)


def _shift(p, prev8, next8, mu, first, last):
    tt = p.shape[0]
    rows = lax.broadcasted_iota(i32, p.shape, 0)
    prev_row = jnp.where(first, 0.0, prev8[SUBLANES - 1:SUBLANES])
    next_row = jnp.where(last, 0.0, next8[0:1])
    pm = jnp.where(rows == 0, prev_row, pltpu.roll(p, 1, axis=0))
    nx = jnp.where(rows == tt - 1, next_row, pltpu.roll(p, tt - 1, axis=0))
    return p + mu * (0.5 * (pm + nx) - p)


def _rw_prep_body(r_ref, rp_ref, rn_ref, k_ref, kp_ref, kn_ref, v_ref, vp_ref, vn_ref,
                  l_ref, lp_ref, ln_ref, mur_ref, muk_ref, muv_ref, mul_ref,
                  w0_ref, w2_ref, a0_ref, a2_ref, g2_ref,
                  ro_ref, ko_ref, vo_ref, ao_ref, lwf_ref, lwb_ref, go_ref):
    i = pl.program_id(1)
    first = i == 0
    last = i == pl.num_programs(1) - 1
    r = _shift(r_ref[0], rp_ref[0], rn_ref[0], mur_ref[...], first, last)
    k = _shift(k_ref[0], kp_ref[0], kn_ref[0], muk_ref[...], first, last)
    v = _shift(v_ref[0], vp_ref[0], vn_ref[0], muv_ref[...], first, last)
    lo = _shift(l_ref[0], lp_ref[0], ln_ref[0], mul_ref[...], first, last)
    wd_f = lo[:, 128:192]
    wd_b = lo[:, 192:256]
    ad = lo[:, 256:320]
    gd = lo[:, 320:448]

    def log_decay(wd, d):
        wl = w0_ref[d:d + 1, :] + jnp.dot(jnp.tanh(wd), w2_ref[d], precision=lax.Precision.HIGHEST,
                                          preferred_element_type=f32)
        return -jnp.exp(-_softplus(-wl) - 0.5)

    lw_f = log_decay(wd_f, 0)
    lw_b = log_decay(wd_b, 1)
    a = _sigmoid(a0_ref[...] + _dot(ad, a2_ref[...]))
    go_ref[0] = _dot(_sigmoid(gd), g2_ref[...])
    for hp in range(RW_PAIRS):
        sl = slice(hp * LANES, (hp + 1) * LANES)
        ro_ref[0, hp] = r[:, sl]
        ko_ref[0, hp] = k[:, sl]
        vo_ref[0, hp] = v[:, sl]
        ao_ref[0, hp] = a[:, sl]
        lwf_ref[0, hp] = lw_f[:, sl]
        lwb_ref[0, hp] = lw_b[:, sl]


def _rw_prep(proj3, mu_r, mu_k, mu_v, mu_l, w0, w2, a0, a2, g2):
    B, T, _ = proj3.shape
    tt = min(256, T)
    nt = T // tt
    g8 = tt // SUBLANES
    n8 = T // SUBLANES

    def tile(width, blk):
        return [pl.BlockSpec((1, tt, width), lambda b, i: (b, i, blk)),
                pl.BlockSpec((1, SUBLANES, width), lambda b, i: (b, jnp.maximum(i * g8 - 1, 0), blk)),
                pl.BlockSpec((1, SUBLANES, width), lambda b, i: (b, jnp.minimum((i + 1) * g8, n8 - 1), blk))]

    full = lambda shape: pl.BlockSpec(shape, lambda b, i: (0,) * len(shape))
    in_specs = (tile(D_MODEL, 3) + tile(D_MODEL, 4) + tile(D_MODEL, 5) + tile(512, SMALL_BLK)
                + [full((1, RW_W))] * 3 + [full((1, 512)), full((2, RW_W)), full((2, 64, RW_W)),
                                           full((1, RW_W)), full((64, RW_W)), full((128, RW_W))])
    hm = jax.ShapeDtypeStruct((B, RW_PAIRS, T, LANES), f32)
    hm_spec = pl.BlockSpec((1, RW_PAIRS, tt, LANES), lambda b, i: (b, 0, i, 0))
    return pl.pallas_call(
        _rw_prep_body,
        grid=(B, nt),
        in_specs=in_specs,
        out_specs=[hm_spec] * 6 + [pl.BlockSpec((1, tt, RW_W), lambda b, i: (b, i, 0))],
        out_shape=[hm] * 6 + [jax.ShapeDtypeStruct((B, T, RW_W), f32)],
        compiler_params=_cparams(("parallel", "parallel")),
        name="rw_prep",
    )(*([proj3] * 12), mu_r, mu_k, mu_v, mu_l, w0, w2, a0, a2, g2)


def _chunk_cumsum(x, reverse):
    L = x.shape[0]
    rows = lax.broadcasted_iota(i32, x.shape, 0)
    s = 1
    while s < L:
        if reverse:
            x = x + jnp.where(rows < L - s, pltpu.roll(x, L - s, axis=0), 0.0)
        else:
            x = x + jnp.where(rows >= s, pltpu.roll(x, s, axis=0), 0.0)
        s *= 2
    return x


def _rw_units(units, rowi, coli):
    L = units[0][0].shape[0]
    n = len(units)
    lhs, rhs, tail, scale = [], [], [], []
    for r, kp, v, kk, bh, lw, cum, S, reverse in units:
        tot = cum[0:1] if reverse else cum[L - 1:L]
        gi = jnp.exp(-cum)
        gl = jnp.exp(tot - cum)
        lhs.append(jnp.concatenate([-kk * jnp.exp(cum - lw), r * jnp.exp(cum)], axis=0))
        rhs.append(jnp.concatenate([bh * gi, kp * gi], axis=0))
        tail.append(jnp.concatenate([bh * gl, kp * gl], axis=0))
        scale.append(jnp.exp(tot))
    G = [_dot_nt(lhs[i], rhs[i]) for i in range(n)]
    sv = [_dot_nt(lhs[i], units[i][7]) for i in range(n)]
    p, a_rb, low = [], [], []
    for i in range(n):
        reverse = units[i][8]
        strict = (coli > rowi) if reverse else (coli < rowi)
        incl = (coli >= rowi) if reverse else (coli <= rowi)
        p.append(jnp.where(strict, G[i][:L, :L], 0.0))
        a_rb.append(jnp.where(incl, G[i][L:, :L], 0.0))
        low.append(jnp.concatenate([jnp.where(strict, G[i][:L, L:], 0.0),
                                    jnp.where(incl, G[i][L:, L:], 0.0)], axis=0))
    av = [_dot(low[i], units[i][2]) for i in range(n)]
    y = [sv[i][:L] + av[i][:L] for i in range(n)]
    s = 1
    while True:
        y = [y[i] + _dot(p[i], y[i]) for i in range(n)]
        s *= 2
        if s >= L:
            break
        p = [_dot(p[i], p[i]) for i in range(n)]
    ru = [_dot(a_rb[i], y[i]) for i in range(n)]
    upd = [_dot_tn(jnp.concatenate([y[i], units[i][2]], axis=0), tail[i]) for i in range(n)]
    outs = [sv[i][L:] + av[i][L:] + ru[i] for i in range(n)]
    states = [units[i][7] * scale[i] + upd[i] for i in range(n)]
    return outs, states


def _rw_scan_body(rf_ref, kf_ref, vf_ref, af_ref, lwf_ref, rb_ref, kb_ref, vb_ref, ab_ref, lwb_ref,
                  kk_ref, ka_ref, of_ref, ob_ref, s_ref):
    L = RW_CHUNK
    H = RW_HEAD

    @pl.when(pl.program_id(1) == 0)
    def _():
        s_ref[...] = jnp.zeros_like(s_ref)

    rowi = lax.broadcasted_iota(i32, (L, L), 0)
    coli = lax.broadcasted_iota(i32, (L, L), 1)

    dirs = ((rf_ref, kf_ref, vf_ref, af_ref, lwf_ref), (rb_ref, kb_ref, vb_ref, ab_ref, lwb_ref))

    def group(it, carry):
        units = []
        for q in range(RW_GROUP):
            hp = it * RW_GROUP + q
            k_k = kk_ref[hp]
            k_a = ka_ref[hp]
            for d, (r_ref, k_ref, v_ref, a_ref, lw_ref) in enumerate(dirs):
                r = r_ref[0, hp]
                k = k_ref[0, hp]
                v = v_ref[0, hp]
                a = a_ref[0, hp]
                lw = lw_ref[0, hp]
                cum = _chunk_cumsum(lw, d == 1)
                kkr = k * k_k
                kp = k * (1.0 + (a - 1.0) * k_a)
                for j in range(2):
                    sl = slice(j * H, (j + 1) * H)
                    kkj = kkr[:, sl]
                    nrm = jnp.sqrt(jnp.sum(kkj * kkj, axis=-1, keepdims=True))
                    kkj = kkj / jnp.maximum(nrm, L2_EPS)
                    units.append((r[:, sl], kp[:, sl], v[:, sl], kkj, kkj * a[:, sl], lw[:, sl],
                                  cum[:, sl], s_ref[d, 2 * hp + j], d == 1))
        outs, states = _rw_units(units, rowi, coli)
        u = 0
        for q in range(RW_GROUP):
            hp = it * RW_GROUP + q
            for d, o_ref in enumerate((of_ref, ob_ref)):
                o_ref[0, hp] = jnp.concatenate([outs[u], outs[u + 1]], axis=1)
                s_ref[d, 2 * hp] = states[u]
                s_ref[d, 2 * hp + 1] = states[u + 1]
                u += 2
        return carry

    lax.fori_loop(0, RW_PAIRS // RW_GROUP, group, 0)


def _rw_scan(r, k, v, a, lwf, lwb, k_k, k_a):
    B, _, T, _ = r.shape
    L = RW_CHUNK
    nc = T // L
    fs = pl.BlockSpec((1, RW_PAIRS, L, LANES), lambda b, c: (b, 0, c, 0))
    bs = pl.BlockSpec((1, RW_PAIRS, L, LANES), lambda b, c: (b, 0, nc - 1 - c, 0))
    ps = pl.BlockSpec((RW_PAIRS, 1, LANES), lambda b, c: (0, 0, 0))
    hm = jax.ShapeDtypeStruct((B, RW_PAIRS, T, LANES), f32)
    return pl.pallas_call(
        _rw_scan_body,
        grid=(B, nc),
        in_specs=[fs] * 5 + [bs] * 5 + [ps, ps],
        out_specs=[fs, bs],
        out_shape=[hm, hm],
        scratch_shapes=[pltpu.VMEM((2, RW_HEADS, RW_HEAD, RW_HEAD), f32)],
        compiler_params=_cparams(("parallel", "arbitrary")),
        name="rw_scan",
    )(r, k, v, a, lwf, r, k, v, a, lwb, k_k, k_a)


def _rw_post_body(of_ref, ob_ref, r_ref, k_ref, v_ref, a_ref, g_ref, ka_ref, rk_ref, gg_ref, gb_ref, o_ref):
    H = RW_HEAD
    for hp in range(RW_PAIRS):
        out = of_ref[0, hp] + ob_ref[0, hp]
        r = r_ref[0, hp]
        v = v_ref[0, hp]
        kp = k_ref[0, hp] * (1.0 + (a_ref[0, hp] - 1.0) * ka_ref[hp])
        rk = r * kp * rk_ref[hp]
        halves = []
        for j in range(2):
            sl = slice(j * H, (j + 1) * H)
            oj = out[:, sl]
            mean = jnp.mean(oj, axis=-1, keepdims=True)
            cj = oj - mean
            var = jnp.mean(cj * cj, axis=-1, keepdims=True)
            nj = cj * lax.rsqrt(var + RW_GN_EPS)
            bonus = jnp.sum(rk[:, sl], axis=-1, keepdims=True) * v[:, sl]
            halves.append((nj, bonus))
        normed = jnp.concatenate([halves[0][0], halves[1][0]], axis=1)
        bonus = jnp.concatenate([halves[0][1], halves[1][1]], axis=1)
        lanes = slice(hp * LANES, (hp + 1) * LANES)
        o_ref[0, :, lanes] = (normed * gg_ref[hp] + gb_ref[hp] + bonus) * g_ref[0, :, lanes]


def _rw_post(out_f, out_b, r, k, v, a, g, k_a, r_k, gn_g, gn_b):
    B, _, T, _ = r.shape
    tt = min(256, T)
    hs = pl.BlockSpec((1, RW_PAIRS, tt, LANES), lambda b, i: (b, 0, i, 0))
    ps = pl.BlockSpec((RW_PAIRS, 1, LANES), lambda b, i: (0, 0, 0))
    ts = pl.BlockSpec((1, tt, RW_W), lambda b, i: (b, i, 0))
    return pl.pallas_call(
        _rw_post_body,
        grid=(B, T // tt),
        in_specs=[hs] * 6 + [ts] + [ps] * 4,
        out_specs=ts,
        out_shape=jax.ShapeDtypeStruct((B, T, RW_W), f32),
        compiler_params=_cparams(("parallel", "parallel")),
        name="rw_post",
    )(out_f, out_b, r, k, v, a, g, k_a, r_k, gn_g, gn_b)


def _merge_body(ml_ref, rw_ref, ga_ref, gb_ref, x_ref, wa_ref, wb_ref, wo_ref, nf_ref, wr_ref, br_ref,
                x1_ref, xn_ref, lg_ref):
    y_a = jnp.dot(ml_ref[...].astype(bf16), wa_ref[...], preferred_element_type=f32)
    y_b = jnp.dot(rw_ref[...].astype(bf16), wb_ref[...], preferred_element_type=f32)
    merged = _sigmoid(ga_ref[...]) * y_a + _sigmoid(gb_ref[...]) * y_b
    x1 = x_ref[...] + jnp.dot(merged.astype(bf16), wo_ref[...], preferred_element_type=f32)
    x1_ref[...] = x1
    xn = x1 * lax.rsqrt(jnp.mean(x1 * x1, axis=-1, keepdims=True) + NORM_EPS) * nf_ref[...]
    xn_ref[...] = xn
    lg_ref[...] = jnp.dot(xn, wr_ref[...], precision=lax.Precision.HIGHEST,
                          preferred_element_type=f32) + br_ref[...]


def _merge(ml_out, rw_out, proj, x2d, w_up_a, w_up_b, w_out, norm_ffn, w_router, b_router):
    n = x2d.shape[0]
    tm = min(512, n)
    row = lambda blk: pl.BlockSpec((tm, D_MODEL), lambda i: (i, blk))
    full = lambda shape: pl.BlockSpec(shape, lambda i: (0,) * len(shape))
    return pl.pallas_call(
        _merge_body,
        grid=(n // tm,),
        in_specs=[row(0), row(0), row(6), row(7), row(0),
                  full((D_MODEL, D_MODEL)), full((D_MODEL, D_MODEL)), full((D_MODEL, D_MODEL)),
                  full((1, D_MODEL)), full((D_MODEL, LANES)), full((1, LANES))],
        out_specs=[row(0), row(0), pl.BlockSpec((tm, LANES), lambda i: (i, 0))],
        out_shape=[jax.ShapeDtypeStruct((n, D_MODEL), f32), jax.ShapeDtypeStruct((n, D_MODEL), f32),
                   jax.ShapeDtypeStruct((n, LANES), f32)],
        compiler_params=_cparams(("parallel",)),
        name="merge",
    )(ml_out, rw_out, proj, proj, x2d, w_up_a, w_up_b, w_out, norm_ffn, w_router, b_router)


def _first_argmax(p, width):
    lane = lax.broadcasted_iota(i32, p.shape, 1)
    top = jnp.max(p, axis=-1, keepdims=True)
    idx = jnp.min(jnp.where(p == top, lane, width), axis=-1, keepdims=True)
    return top, idx, lane


def _route_body(lg_ref, meta_ref, gate_ref, cnt_ref, carry_ref):
    @pl.when(pl.program_id(0) == 0)
    def _():
        carry_ref[...] = jnp.zeros_like(carry_ref)

    lg = lg_ref[...]
    tr = lg.shape[0]
    gl = lg[:, 0:N_GROUPS]
    ge = jnp.exp(gl - jnp.max(gl, axis=-1, keepdims=True))
    p_group = ge / jnp.sum(ge, axis=-1, keepdims=True)
    p_g, g_idx, _ = _first_argmax(p_group, N_GROUPS)
    el = jnp.zeros((tr, EXPERTS_PER_GROUP), f32)
    for g in range(N_GROUPS):
        lo = N_GROUPS + g * EXPERTS_PER_GROUP
        el = el + jnp.where(g_idx == g, 1.0, 0.0) * lg[:, lo:lo + EXPERTS_PER_GROUP]
    ee = jnp.exp(el - jnp.max(el, axis=-1, keepdims=True))
    p_exp = ee / jnp.sum(ee, axis=-1, keepdims=True)
    p1, i1, lane8 = _first_argmax(p_exp, EXPERTS_PER_GROUP)
    p2, i2, _ = _first_argmax(jnp.where(lane8 == i1, -1.0, p_exp), EXPERTS_PER_GROUP)
    psum = p1 + p2
    g1 = p_g * p1 / psum
    g2 = p_g * p2 / psum
    e0 = g_idx * EXPERTS_PER_GROUP + i1
    e1 = g_idx * EXPERTS_PER_GROUP + i2
    lane = lax.broadcasted_iota(i32, (tr, LANES), 1)
    hit0 = lane == e0
    hit1 = lane == e1
    onehot = jnp.where(hit0 | hit1, 1.0, 0.0)
    rowi = lax.broadcasted_iota(i32, (tr, tr), 0)
    coli = lax.broadcasted_iota(i32, (tr, tr), 1)
    before = jnp.where(coli < rowi, 1.0, 0.0)
    prefix = _dot(before, onehot) + carry_ref[...]
    r0 = jnp.sum(jnp.where(hit0, prefix, 0.0), axis=-1, keepdims=True).astype(i32)
    r1 = jnp.sum(jnp.where(hit1, prefix, 0.0), axis=-1, keepdims=True).astype(i32)
    carry = carry_ref[...] + jnp.sum(onehot, axis=0, keepdims=True)
    carry_ref[...] = carry
    cnt_ref[...] = carry.astype(i32)
    meta_ref[...] = jnp.where(lane == 0, e0, jnp.where(lane == 1, e1, jnp.where(lane == 2, r0, r1)))
    gate_ref[...] = jnp.where(lane == 0, g1, g2)


def _route(logits):
    n = logits.shape[0]
    tr = min(512, n)
    return pl.pallas_call(
        _route_body,
        grid=(n // tr,),
        in_specs=[pl.BlockSpec((tr, LANES), lambda i: (i, 0))],
        out_specs=[pl.BlockSpec((tr, LANES), lambda i: (i, 0)), pl.BlockSpec((tr, LANES), lambda i: (i, 0)),
                   pl.BlockSpec((1, LANES), lambda i: (0, 0))],
        out_shape=[jax.ShapeDtypeStruct((n, LANES), i32), jax.ShapeDtypeStruct((n, LANES), f32),
                   jax.ShapeDtypeStruct((1, LANES), i32)],
        scratch_shapes=[pltpu.VMEM((1, LANES), f32)],
        compiler_params=_cparams(("arbitrary",)),
        name="route",
    )(logits)


def _scatter_body(pstart_ref, e0_ref, e1_ref, r0_ref, r1_ref, x_ref, buf_in, xs_hbm, sem):
    del buf_in
    ts = e0_ref.shape[0]

    def row_copy(t, dst):
        return pltpu.make_async_copy(x_ref.at[pl.ds(t, 1)], xs_hbm.at[pl.ds(dst, 1)], sem)

    def issue(t, c):
        row_copy(t, pstart_ref[e0_ref[t]] + r0_ref[t]).start()
        row_copy(t, pstart_ref[e1_ref[t]] + r1_ref[t]).start()
        return c

    lax.fori_loop(0, ts, issue, 0)

    def drain(t, c):
        row_copy(0, 0).wait()
        row_copy(0, 0).wait()
        return c

    lax.fori_loop(0, ts, drain, 0)


def _scatter(xn, pstart, e0, e1, r0, r1, n_rows):
    n = xn.shape[0]
    ts = min(512, n)
    tok = pl.BlockSpec((ts,), lambda i, ps: (i,), memory_space=pltpu.SMEM)
    anyspec = pl.BlockSpec(memory_space=pl.ANY)
    buf = jnp.zeros((n_rows, D_MODEL), f32)
    return pl.pallas_call(
        _scatter_body,
        grid_spec=pltpu.PrefetchScalarGridSpec(
            num_scalar_prefetch=1, grid=(n // ts,),
            in_specs=[tok, tok, tok, tok, pl.BlockSpec((ts, D_MODEL), lambda i, ps: (i, 0)), anyspec],
            out_specs=anyspec,
            scratch_shapes=[pltpu.SemaphoreType.DMA(())]),
        out_shape=jax.ShapeDtypeStruct((n_rows, D_MODEL), f32),
        input_output_aliases={6: 0},
        compiler_params=_cparams(("arbitrary",)),
        name="moe_scatter",
    )(pstart, e0, e1, r0, r1, xn, buf)


def _experts_body(be_ref, nu_ref, xs_ref, wg_ref, wu_ref, wd_ref, ys_ref):
    del be_ref

    @pl.when(pl.program_id(0) < nu_ref[0])
    def _():
        xb = xs_ref[...].astype(bf16)
        hg = jnp.dot(xb, wg_ref[0], preferred_element_type=f32)
        hu = jnp.dot(xb, wu_ref[0], preferred_element_type=f32)
        hb = hg * _sigmoid(hg) * hu
        ys_ref[...] = jnp.dot(hb.astype(bf16), wd_ref[0], preferred_element_type=f32)

    @pl.when(pl.program_id(0) >= nu_ref[0])
    def _():
        ys_ref[...] = jnp.zeros_like(ys_ref)


def _experts(xs, block_e, n_used, w_gate, w_up, w_down):
    n_rows = xs.shape[0]
    nb = n_rows // MOE_BLOCK
    return pl.pallas_call(
        _experts_body,
        grid_spec=pltpu.PrefetchScalarGridSpec(
            num_scalar_prefetch=2, grid=(nb,),
            in_specs=[pl.BlockSpec((MOE_BLOCK, D_MODEL), lambda i, be, nu: (i, 0)),
                      pl.BlockSpec((1, D_MODEL, D_EXPERT), lambda i, be, nu: (be[i], 0, 0)),
                      pl.BlockSpec((1, D_MODEL, D_EXPERT), lambda i, be, nu: (be[i], 0, 0)),
                      pl.BlockSpec((1, D_EXPERT, D_MODEL), lambda i, be, nu: (be[i], 0, 0))],
            out_specs=pl.BlockSpec((MOE_BLOCK, D_MODEL), lambda i, be, nu: (i, 0))),
        out_shape=jax.ShapeDtypeStruct((n_rows, D_MODEL), f32),
        compiler_params=_cparams(("arbitrary",)),
        name="moe_experts",
    )(block_e, n_used, xs, w_gate, w_up, w_down)


def _combine_body(pstart_ref, e0_ref, e1_ref, r0_ref, r1_ref, x1_ref, gate_ref, nf_ref, ys_hbm,
                  o_ref, y0_ref, y1_ref, sem):
    tc = e0_ref.shape[0]

    def row_copy(src, dst_ref, t):
        return pltpu.make_async_copy(ys_hbm.at[pl.ds(src, 1)], dst_ref.at[pl.ds(t, 1)], sem)

    def issue(t, c):
        row_copy(pstart_ref[e0_ref[t]] + r0_ref[t], y0_ref, t).start()
        row_copy(pstart_ref[e1_ref[t]] + r1_ref[t], y1_ref, t).start()
        return c

    lax.fori_loop(0, tc, issue, 0)

    def drain(t, c):
        row_copy(0, y0_ref, 0).wait()
        row_copy(0, y1_ref, 0).wait()
        return c

    lax.fori_loop(0, tc, drain, 0)
    gates = gate_ref[...]
    x2 = x1_ref[...] + gates[:, 0:1] * y0_ref[...] + gates[:, 1:2] * y1_ref[...]
    o_ref[...] = x2 * lax.rsqrt(jnp.mean(x2 * x2, axis=-1, keepdims=True) + NORM_EPS) * nf_ref[...]


def _combine(x1, gates, norm_final, ys, pstart, e0, e1, r0, r1):
    n = x1.shape[0]
    tc = min(256, n)
    tok = pl.BlockSpec((tc,), lambda i, ps: (i,), memory_space=pltpu.SMEM)
    return pl.pallas_call(
        _combine_body,
        grid_spec=pltpu.PrefetchScalarGridSpec(
            num_scalar_prefetch=1, grid=(n // tc,),
            in_specs=[tok, tok, tok, tok,
                      pl.BlockSpec((tc, D_MODEL), lambda i, ps: (i, 0)),
                      pl.BlockSpec((tc, LANES), lambda i, ps: (i, 0)),
                      pl.BlockSpec((1, D_MODEL), lambda i, ps: (0, 0)),
                      pl.BlockSpec(memory_space=pl.ANY)],
            out_specs=pl.BlockSpec((tc, D_MODEL), lambda i, ps: (i, 0)),
            scratch_shapes=[pltpu.VMEM((tc, D_MODEL), f32), pltpu.VMEM((tc, D_MODEL), f32),
                            pltpu.SemaphoreType.DMA(())]),
        out_shape=jax.ShapeDtypeStruct((n, D_MODEL), f32),
        compiler_params=_cparams(("arbitrary",)),
        name="moe_combine",
    )(pstart, e0, e1, r0, r1, x1, gates, norm_final, ys)


def _prep_weights(w_in, rw_mu, w_router_group, b_router_group, w_router_expert, b_router_expert):
    rw0 = ML_COLS
    g0 = ML_COLS + RW_COLS
    zeros = lambda c: jnp.zeros((D_MODEL, c), w_in.dtype)
    w_all = jnp.concatenate([
        w_in[:, 0:3072],
        w_in[:, rw0:rw0 + 3072],
        w_in[:, g0:g0 + 2048],
        w_in[:, 3072:3088], zeros(112),
        w_in[:, rw0 + 3072:rw0 + 3392], zeros(64)], axis=1).astype(bf16)
    mu_r = rw_mu[None, 0:1024]
    mu_k = rw_mu[None, 1024:2048]
    mu_v = rw_mu[None, 2048:3072]
    mu_l = jnp.concatenate([jnp.zeros((128,), f32), rw_mu[3072:3392], jnp.zeros((64,), f32)])[None]
    w_router = jnp.concatenate([w_router_group, w_router_expert,
                                jnp.zeros((D_MODEL, LANES - N_GROUPS - N_EXPERTS), f32)], axis=1)
    b_router = jnp.concatenate([b_router_group, b_router_expert,
                                jnp.zeros((LANES - N_GROUPS - N_EXPERTS,), f32)])[None]
    return w_all, mu_r, mu_k, mu_v, mu_l, w_router, b_router


def _pairs(p):
    return p.reshape(RW_PAIRS, 1, LANES)


def _forward(x, norm_mix, w_all, mus, ml_b_i, ml_b_f, ml_norm_g, rw_w0, rw_w2, rw_a0, rw_a2, rw_g2,
             rw_k_k, rw_k_a, rw_r_k, rw_gn_g, rw_gn_b, w_up_a, w_up_b, w_out, norm_ffn,
             w_router, b_router, w_gate, w_up, w_down, norm_final):
    B, T, _ = x.shape
    n = B * T
    x2d = x.reshape(n, D_MODEL)
    proj = _proj(x2d, norm_mix[None], w_all)
    proj3 = proj.reshape(B, T, W_ALL)
    gates_t = jnp.swapaxes(proj3[:, :, 8 * D_MODEL:8 * D_MODEL + 16], 1, 2)
    ml_out = _mlstm(proj3, gates_t, ml_b_i, ml_b_f, ml_norm_g[None])
    mu_r, mu_k, mu_v, mu_l = mus
    r, k, v, a, lwf, lwb, g = _rw_prep(proj3, mu_r, mu_k, mu_v, mu_l, rw_w0, rw_w2, rw_a0[None], rw_a2, rw_g2)
    out_f, out_b = _rw_scan(r, k, v, a, lwf, lwb, _pairs(rw_k_k), _pairs(rw_k_a))
    rw_out = _rw_post(out_f, out_b, r, k, v, a, g, _pairs(rw_k_a), _pairs(rw_r_k), _pairs(rw_gn_g),
                      _pairs(rw_gn_b))
    x1, xn, logits = _merge(ml_out.reshape(n, ML_V_W), rw_out.reshape(n, RW_W), proj, x2d,
                            w_up_a, w_up_b, w_out, norm_ffn[None], w_router, b_router)
    meta, gates, counts = _route(logits)
    counts = counts[0, :N_EXPERTS]
    padded = ((counts + MOE_BLOCK - 1) // MOE_BLOCK) * MOE_BLOCK
    pend = jnp.cumsum(padded)
    pstart = (pend - padded).astype(i32)
    nb = (2 * n) // MOE_BLOCK + N_EXPERTS
    block_start = jnp.arange(nb, dtype=i32) * MOE_BLOCK
    block_e = jnp.minimum(jnp.sum(block_start[:, None] >= pend[None, :], axis=1), N_EXPERTS - 1).astype(i32)
    n_used = (pend[-1:] // MOE_BLOCK).astype(i32)
    e0, e1, r0, r1 = meta[:, 0], meta[:, 1], meta[:, 2], meta[:, 3]
    xs = _scatter(xn, pstart, e0, e1, r0, r1, nb * MOE_BLOCK)
    ys = _experts(xs, block_e, n_used, w_gate, w_up, w_down)
    y = _combine(x1, gates, norm_final[None], ys, pstart, e0, e1, r0, r1)
    return y.reshape(B, T, D_MODEL)


def kernel(x_prompt, x_sample, norm_mix, w_in, ml_b_i, ml_b_f, ml_norm_g, rw_mu, rw_w0, rw_w2, rw_a0, rw_a2,
           rw_g2, rw_k_k, rw_k_a, rw_r_k, rw_gn_g, rw_gn_b, w_up_a, w_up_b, w_out, norm_ffn,
           w_router_group, b_router_group, w_router_expert, b_router_expert,
           w_expert_gate, w_expert_up, w_expert_down, norm_final):
    w_all, mu_r, mu_k, mu_v, mu_l, w_router, b_router = _prep_weights(
        w_in[0], rw_mu[0], w_router_group[0], b_router_group[0], w_router_expert[0], b_router_expert[0])
    args = (norm_mix[0], w_all, (mu_r, mu_k, mu_v, mu_l), ml_b_i[0], ml_b_f[0], ml_norm_g[0],
            rw_w0[0], rw_w2[0], rw_a0[0], rw_a2[0], rw_g2[0], rw_k_k[0], rw_k_a[0], rw_r_k[0].reshape(-1),
            rw_gn_g[0], rw_gn_b[0], w_up_a[0].astype(bf16), w_up_b[0].astype(bf16), w_out[0].astype(bf16),
            norm_ffn[0], w_router, b_router, w_expert_gate[0].astype(bf16), w_expert_up[0].astype(bf16),
            w_expert_down[0].astype(bf16), norm_final)
    return (_forward(x_prompt, *args), _forward(x_sample, *args))
```

```python
import functools

import jax
import jax.numpy as jnp
from jax import lax
from jax.experimental import pallas as pl
from jax.experimental.pallas import tpu as pltpu

f32 = jnp.float32
bf16 = jnp.bfloat16
i32 = jnp.int32

D_MODEL = 1024
ML_HEADS = 4
ML_DQK = 128
ML_DV = 256
ML_QK_W = ML_HEADS * ML_DQK
ML_V_W = ML_HEADS * ML_DV
ML_COLS = 2 * ML_QK_W + 2 * ML_V_W + 4 * ML_HEADS
RW_HEAD = 64
RW_HEADS = 16
RW_W = 1024
RW_PAIRS = RW_HEADS // 2
RW_COLS = 3 * RW_W + 64 + 64 + 64 + 128
N_GROUPS = 8
EXPERTS_PER_GROUP = 8
N_EXPERTS = 64
D_EXPERT = 512
NORM_EPS = 1e-6
RW_GN_EPS = 64e-5
L2_EPS = 1e-12

LANES = 128
SUBLANES = 8
W_ALL = 8 * D_MODEL + 512
SMALL_BLK = (8 * D_MODEL) // 512
ML_CHUNK = 128
RW_CHUNK = 64
RW_GROUP = 8
MOE_BLOCK = 256
DMA_UNROLL = 8
VMEM_LIMIT = 56 * 1024 * 1024


def _cparams(sem):
    return pltpu.CompilerParams(dimension_semantics=sem, vmem_limit_bytes=VMEM_LIMIT)


def _dot(a, b):
    return jnp.dot(a.astype(bf16), b.astype(bf16), preferred_element_type=f32)


def _dot_nt(a, b):
    return lax.dot_general(a.astype(bf16), b.astype(bf16), (((1,), (1,)), ((), ())),
                           preferred_element_type=f32)


def _dot_tn(a, b):
    return lax.dot_general(a.astype(bf16), b.astype(bf16), (((0,), (0,)), ((), ())),
                           preferred_element_type=f32)


def _sigmoid(x):
    return 1.0 / (1.0 + jnp.exp(-x))


def _softplus(x):
    return jnp.maximum(x, 0.0) + jnp.log1p(jnp.exp(-jnp.abs(x)))


def _proj_body(x_ref, g_ref, w_ref, o_ref, xn_ref):
    @pl.when(pl.program_id(1) == 0)
    def _():
        x = x_ref[...]
        ms = jnp.mean(x * x, axis=-1, keepdims=True)
        xn_ref[...] = (x * lax.rsqrt(ms + NORM_EPS) * g_ref[...]).astype(bf16)

    o_ref[...] = jnp.dot(xn_ref[...], w_ref[...], preferred_element_type=f32)


def _proj(x2d, g, w_all):
    n = x2d.shape[0]
    tm = min(1024, n)
    tn = W_ALL // 4
    return pl.pallas_call(
        _proj_body,
        grid=(n // tm, W_ALL // tn),
        in_specs=[pl.BlockSpec((tm, D_MODEL), lambda i, j: (i, 0)),
                  pl.BlockSpec((1, D_MODEL), lambda i, j: (0, 0)),
                  pl.BlockSpec((D_MODEL, tn), lambda i, j: (0, j))],
        out_specs=pl.BlockSpec((tm, tn), lambda i, j: (i, j)),
        out_shape=jax.ShapeDtypeStruct((n, W_ALL), f32),
        scratch_shapes=[pltpu.VMEM((tm, D_MODEL), bf16)],
        compiler_params=_cparams(("parallel", "arbitrary")),
        name="proj",
    )(x2d, g, w_all)


def _log_sigmoid(x):
    return jnp.minimum(x, 0.0) - jnp.log1p(jnp.exp(-jnp.abs(x)))


def _mlstm_body(br_ref, bc_ref, qf_ref, kf_ref, vf_ref, gf_ref, gtf_ref,
                qb_ref, kb_ref, vb_ref, gb_ref, gtb_ref, hf_ref, hb_ref, c_ref, n_ref, m_ref):
    L = ML_CHUNK

    @pl.when(pl.program_id(1) == 0)
    def _():
        c_ref[...] = jnp.zeros_like(c_ref)
        n_ref[...] = jnp.zeros_like(n_ref)
        m_ref[...] = jnp.zeros_like(m_ref)

    rowi = lax.broadcasted_iota(i32, (L, L), 0)
    coli = lax.broadcasted_iota(i32, (L, L), 1)
    hp = lax.Precision.HIGHEST
    scale = ML_DQK ** -0.5
    units = []
    for d, (q_ref, k_ref, v_ref, g_ref, gt_ref, out_ref) in enumerate(
            ((qf_ref, kf_ref, vf_ref, gf_ref, gtf_ref, hf_ref), (qb_ref, kb_ref, vb_ref, gb_ref, gtb_ref, hb_ref))):
        mask = (coli >= rowi) if d else (coli <= rowi)
        tri = jnp.where(mask, 1.0, 0.0)
        g_all = g_ref[0] + br_ref[...]
        gt_all = gt_ref[0] + bc_ref[...]
        bcol_all = jnp.dot(tri, _log_sigmoid(g_all), precision=hp, preferred_element_type=f32)
        brow_all = lax.dot_general(_log_sigmoid(gt_all), tri, (((1,), (1,)), ((), ())), precision=hp,
                                   preferred_element_type=f32)
        for h in range(ML_HEADS):
            ii = 4 * d + h
            fi = 8 + 4 * d + h
            units.append(dict(
                d=d, h=h, mask=mask, out_ref=out_ref,
                i_col=g_all[:, ii:ii + 1], i_row=gt_all[ii:ii + 1, :],
                b_col=bcol_all[:, fi:fi + 1], b_row=brow_all[fi:fi + 1, :],
                q=q_ref[0, :, h * ML_DQK:(h + 1) * ML_DQK] * scale,
                k=k_ref[0, :, h * ML_DQK:(h + 1) * ML_DQK],
                v=v_ref[0, :, h * ML_DV:(h + 1) * ML_DV],
                c=c_ref[d, h], n=n_ref[d, h], m=m_ref[d, h]))
    for u in units:
        logw = jnp.where(u['mask'], u['b_col'] - u['b_row'] + u['i_row'], -jnp.inf)
        m_inter = u['b_col'] + u['m']
        u['m_t'] = jnp.maximum(m_inter, jnp.max(logw, axis=1, keepdims=True))
        u['dm'] = jnp.exp(logw - u['m_t'])
        u['inter'] = jnp.exp(m_inter - u['m_t'])
    qk = [_dot_nt(u['q'], u['k']) for u in units]
    s = [qk[i] * u['dm'] for i, u in enumerate(units)]
    sv = [_dot(s[i], u['v']) for i, u in enumerate(units)]
    qc = [_dot(u['q'], u['c']) for u in units]
    for i, u in enumerate(units):
        den = (jnp.sum(s[i], axis=1, keepdims=True)
               + u['inter'] * jnp.sum(u['q'] * u['n'], axis=1, keepdims=True))
        hh = (sv[i] + u['inter'] * qc[i]) / jnp.maximum(jnp.abs(den), jnp.exp(-u['m_t']))
        u['out_ref'][0, :, u['h'] * ML_DV:(u['h'] + 1) * ML_DV] = hh
        b_last = u['b_col'][0:1] if u['d'] else u['b_col'][L - 1:L]
        logu_col = b_last - u['b_col'] + u['i_col']
        logu_row = b_last - u['b_row'] + u['i_row']
        u['m_new'] = jnp.maximum(b_last + u['m'], jnp.max(logu_row, axis=1, keepdims=True))
        u['decay'] = jnp.exp(b_last + u['m'] - u['m_new'])
        u['ku'] = u['k'] * jnp.exp(logu_col - u['m_new'])
    kv = [_dot_tn(u['ku'], u['v']) for u in units]
    for i, u in enumerate(units):
        d, h = u['d'], u['h']
        c_ref[d, h] = u['decay'] * u['c'] + kv[i]
        n_ref[d, h] = u['decay'] * u['n'] + jnp.sum(u['ku'], axis=0, keepdims=True)
        m_ref[d, h] = u['m_new']


def _mlstm(proj3, gates_t, bias_row, bias_col):
    B, T, _ = proj3.shape
    L = ML_CHUNK
    nc = T // L

    def specs(cm):
        return [pl.BlockSpec((1, L, ML_QK_W), lambda b, c: (b, cm(c), 0)),
                pl.BlockSpec((1, L, ML_QK_W), lambda b, c: (b, cm(c), 1)),
                pl.BlockSpec((1, L, ML_V_W), lambda b, c: (b, cm(c), 1)),
                pl.BlockSpec((1, L, LANES), lambda b, c: (b, cm(c), (8 * D_MODEL) // LANES)),
                pl.BlockSpec((1, 16, L), lambda b, c: (b, 0, cm(c)))]

    fwd = lambda c: c
    bwd = lambda c: nc - 1 - c
    out = jax.ShapeDtypeStruct((B, T, ML_V_W), f32)
    return pl.pallas_call(
        _mlstm_body,
        grid=(B, nc),
        in_specs=[pl.BlockSpec((1, LANES), lambda b, c: (0, 0)), pl.BlockSpec((16, 1), lambda b, c: (0, 0))]
        + specs(fwd) + specs(bwd),
        out_specs=[pl.BlockSpec((1, L, ML_V_W), lambda b, c: (b, c, 0)),
                   pl.BlockSpec((1, L, ML_V_W), lambda b, c: (b, bwd(c), 0))],
        out_shape=[out, out],
        scratch_shapes=[pltpu.VMEM((2, ML_HEADS, ML_DQK, ML_DV), f32),
                        pltpu.VMEM((2, ML_HEADS, 1, ML_DQK), f32),
                        pltpu.VMEM((2, ML_HEADS, 1, 1), f32)],
        compiler_params=_cparams(("parallel", "arbitrary")),
        name="mlstm",
    )(bias_row, bias_col, *([proj3] * 4), gates_t, *([proj3] * 4), gates_t)


def _shift(p, prev8, next8, mu, first, last):
    tt = p.shape[0]
    rows = lax.broadcasted_iota(i32, p.shape, 0)
    prev_row = jnp.where(first, 0.0, prev8[SUBLANES - 1:SUBLANES])
    next_row = jnp.where(last, 0.0, next8[0:1])
    pm = jnp.where(rows == 0, prev_row, pltpu.roll(p, 1, axis=0))
    nx = jnp.where(rows == tt - 1, next_row, pltpu.roll(p, tt - 1, axis=0))
    return p + mu * (0.5 * (pm + nx) - p)


def _rw_prep_body(r_ref, rp_ref, rn_ref, k_ref, kp_ref, kn_ref, v_ref, vp_ref, vn_ref,
                  l_ref, lp_ref, ln_ref, mur_ref, muk_ref, muv_ref, mul_ref,
                  w0_ref, w2_ref, a0_ref, a2_ref, g2_ref,
                  ro_ref, ko_ref, vo_ref, ao_ref, lwf_ref, lwb_ref, go_ref):
    i = pl.program_id(1)
    first = i == 0
    last = i == pl.num_programs(1) - 1
    r = _shift(r_ref[0], rp_ref[0], rn_ref[0], mur_ref[...], first, last)
    k = _shift(k_ref[0], kp_ref[0], kn_ref[0], muk_ref[...], first, last)
    v = _shift(v_ref[0], vp_ref[0], vn_ref[0], muv_ref[...], first, last)
    lo = _shift(l_ref[0], lp_ref[0], ln_ref[0], mul_ref[...], first, last)
    wd_f = lo[:, 128:192]
    wd_b = lo[:, 192:256]
    ad = lo[:, 256:320]
    gd = lo[:, 320:448]

    def log_decay(wd, d):
        wl = w0_ref[d:d + 1, :] + jnp.dot(jnp.tanh(wd), w2_ref[d], precision=lax.Precision.HIGHEST,
                                          preferred_element_type=f32)
        return -jnp.exp(-_softplus(-wl) - 0.5)

    lw_f = log_decay(wd_f, 0)
    lw_b = log_decay(wd_b, 1)
    a = _sigmoid(a0_ref[...] + _dot(ad, a2_ref[...]))
    go_ref[0] = _dot(_sigmoid(gd), g2_ref[...])
    for hp in range(RW_PAIRS):
        sl = slice(hp * LANES, (hp + 1) * LANES)
        ro_ref[0, hp] = r[:, sl]
        ko_ref[0, hp] = k[:, sl]
        vo_ref[0, hp] = v[:, sl]
        ao_ref[0, hp] = a[:, sl]
        lwf_ref[0, hp] = lw_f[:, sl]
        lwb_ref[0, hp] = lw_b[:, sl]


def _rw_prep(proj3, mu_r, mu_k, mu_v, mu_l, w0, w2, a0, a2, g2):
    B, T, _ = proj3.shape
    tt = min(256, T)
    nt = T // tt
    g8 = tt // SUBLANES
    n8 = T // SUBLANES

    def tile(width, blk):
        return [pl.BlockSpec((1, tt, width), lambda b, i: (b, i, blk)),
                pl.BlockSpec((1, SUBLANES, width), lambda b, i: (b, jnp.maximum(i * g8 - 1, 0), blk)),
                pl.BlockSpec((1, SUBLANES, width), lambda b, i: (b, jnp.minimum((i + 1) * g8, n8 - 1), blk))]

    full = lambda shape: pl.BlockSpec(shape, lambda b, i: (0,) * len(shape))
    in_specs = (tile(D_MODEL, 3) + tile(D_MODEL, 4) + tile(D_MODEL, 5) + tile(512, SMALL_BLK)
                + [full((1, RW_W))] * 3 + [full((1, 512)), full((2, RW_W)), full((2, 64, RW_W)),
                                           full((1, RW_W)), full((64, RW_W)), full((128, RW_W))])
    hm = jax.ShapeDtypeStruct((B, RW_PAIRS, T, LANES), f32)
    hm_spec = pl.BlockSpec((1, RW_PAIRS, tt, LANES), lambda b, i: (b, 0, i, 0))
    return pl.pallas_call(
        _rw_prep_body,
        grid=(B, nt),
        in_specs=in_specs,
        out_specs=[hm_spec] * 6 + [pl.BlockSpec((1, tt, RW_W), lambda b, i: (b, i, 0))],
        out_shape=[hm] * 6 + [jax.ShapeDtypeStruct((B, T, RW_W), f32)],
        compiler_params=_cparams(("parallel", "parallel")),
        name="rw_prep",
    )(*([proj3] * 12), mu_r, mu_k, mu_v, mu_l, w0, w2, a0, a2, g2)


def _chunk_cumsum(x, reverse):
    L = x.shape[0]
    rows = lax.broadcasted_iota(i32, x.shape, 0)
    s = 1
    while s < L:
        if reverse:
            x = x + jnp.where(rows < L - s, pltpu.roll(x, L - s, axis=0), 0.0)
        else:
            x = x + jnp.where(rows >= s, pltpu.roll(x, s, axis=0), 0.0)
        s *= 2
    return x


def _block_diag(x, lo):
    return jnp.concatenate([jnp.where(lo, x, 0.0), jnp.where(lo, 0.0, x)], axis=0)


def _rw_units(units, rowi, coli, lo):
    L = units[0][0].shape[0]
    n = len(units)
    lhs, rhs, tail, scale = [], [], [], []
    for r, kp, v, kk, bh, lw, cum, S, reverse in units:
        tot = cum[0:1] if reverse else cum[L - 1:L]
        gi = jnp.exp(-cum)
        gl = jnp.exp(tot - cum)
        lhs.append(jnp.concatenate([-kk * jnp.exp(cum - lw), r * jnp.exp(cum)], axis=0))
        bt = bh * gi
        kt = kp * gi
        rhs.append(jnp.concatenate([_block_diag(bt, lo), _block_diag(kt, lo)], axis=0))
        tail.append(jnp.concatenate([bh * gl, kp * gl], axis=0))
        scale.append(jnp.exp(tot))
    G = [_dot_nt(lhs[i], rhs[i]) for i in range(n)]
    sv = [_dot_nt(lhs[i], units[i][7]) for i in range(n)]
    p, a_rb, low = [], [], []
    for i in range(n):
        reverse = units[i][8]
        strict = (coli > rowi) if reverse else (coli < rowi)
        incl = (coli >= rowi) if reverse else (coli <= rowi)
        p.append(jnp.where(strict, G[i][:L, :2 * L], 0.0))
        a_rb.append(jnp.where(incl, G[i][L:, :2 * L], 0.0))
        low.append(jnp.concatenate([jnp.where(strict, G[i][:L, 2 * L:], 0.0),
                                    jnp.where(incl, G[i][L:, 2 * L:], 0.0)], axis=0))
    av = [_dot(low[i], _block_diag(units[i][2], lo)) for i in range(n)]
    y = [sv[i][:L] + av[i][:L] for i in range(n)]
    s = 1
    while True:
        y = [y[i] + _dot(p[i], _block_diag(y[i], lo)) for i in range(n)]
        s *= 2
        if s >= L:
            break
        p = [_dot(p[i], _block_diag(p[i], lo)) for i in range(n)]
    ru = [_dot(a_rb[i], _block_diag(y[i], lo)) for i in range(n)]
    upd = [_dot_tn(jnp.concatenate([y[i], units[i][2]], axis=0), tail[i]) for i in range(n)]
    outs = [sv[i][L:] + av[i][L:] + ru[i] for i in range(n)]
    states = [units[i][7] * scale[i] + upd[i] for i in range(n)]
    return outs, states


def _rw_scan_body(rf_ref, kf_ref, vf_ref, af_ref, lwf_ref, rb_ref, kb_ref, vb_ref, ab_ref, lwb_ref,
                  kk_ref, ka_ref, of_ref, ob_ref, s_ref):
    L = RW_CHUNK
    H = RW_HEAD

    @pl.when(pl.program_id(1) == 0)
    def _():
        s_ref[...] = jnp.zeros_like(s_ref)

    rowi = lax.broadcasted_iota(i32, (L, 2 * L), 0)
    coli = lax.broadcasted_iota(i32, (L, 2 * L), 1) % L
    lo = lax.broadcasted_iota(i32, (1, LANES), 1) < H
    diag = (lax.broadcasted_iota(i32, (LANES, LANES), 0) < H) == (lax.broadcasted_iota(i32, (LANES, LANES), 1) < H)
    dirs = ((rf_ref, kf_ref, vf_ref, af_ref, lwf_ref), (rb_ref, kb_ref, vb_ref, ab_ref, lwb_ref))

    def group(it, carry):
        units = []
        for q in range(RW_GROUP):
            hp = it * RW_GROUP + q
            k_k = kk_ref[hp]
            k_a = ka_ref[hp]
            for d, (r_ref, k_ref, v_ref, a_ref, lw_ref) in enumerate(dirs):
                r = r_ref[0, hp]
                k = k_ref[0, hp]
                v = v_ref[0, hp]
                a = a_ref[0, hp]
                lw = lw_ref[0, hp]
                cum = _chunk_cumsum(lw, d == 1)
                kk = k * k_k
                sq = kk * kk
                n0 = jnp.sum(jnp.where(lo, sq, 0.0), axis=-1, keepdims=True)
                n1 = jnp.sum(jnp.where(lo, 0.0, sq), axis=-1, keepdims=True)
                kk = kk / jnp.maximum(jnp.sqrt(jnp.where(lo, n0, n1)), L2_EPS)
                kp = k * (1.0 + (a - 1.0) * k_a)
                units.append((r, kp, v, kk, kk * a, lw, cum, s_ref[d, hp], d == 1))
        outs, states = _rw_units(units, rowi, coli, lo)
        u = 0
        for q in range(RW_GROUP):
            hp = it * RW_GROUP + q
            for d, o_ref in enumerate((of_ref, ob_ref)):
                o_ref[0, hp] = outs[u]
                s_ref[d, hp] = jnp.where(diag, states[u], 0.0)
                u += 1
        return carry

    lax.fori_loop(0, RW_PAIRS // RW_GROUP, group, 0)


def _rw_scan(r, k, v, a, lwf, lwb, k_k, k_a):
    B, _, T, _ = r.shape
    L = RW_CHUNK
    nc = T // L
    fs = pl.BlockSpec((1, RW_PAIRS, L, LANES), lambda b, c: (b, 0, c, 0))
    bs = pl.BlockSpec((1, RW_PAIRS, L, LANES), lambda b, c: (b, 0, nc - 1 - c, 0))
    ps = pl.BlockSpec((RW_PAIRS, 1, LANES), lambda b, c: (0, 0, 0))
    hm = jax.ShapeDtypeStruct((B, RW_PAIRS, T, LANES), f32)
    return pl.pallas_call(
        _rw_scan_body,
        grid=(B, nc),
        in_specs=[fs] * 5 + [bs] * 5 + [ps, ps],
        out_specs=[fs, bs],
        out_shape=[hm, hm],
        scratch_shapes=[pltpu.VMEM((2, RW_PAIRS, LANES, LANES), f32)],
        compiler_params=_cparams(("parallel", "arbitrary")),
        name="rw_scan",
    )(r, k, v, a, lwf, r, k, v, a, lwb, k_k, k_a)


def _rw_post_body(of_ref, ob_ref, r_ref, k_ref, v_ref, a_ref, g_ref, ka_ref, rk_ref, gg_ref, gb_ref, o_ref):
    H = RW_HEAD
    for hp in range(RW_PAIRS):
        out = of_ref[0, hp] + ob_ref[0, hp]
        r = r_ref[0, hp]
        v = v_ref[0, hp]
        kp = k_ref[0, hp] * (1.0 + (a_ref[0, hp] - 1.0) * ka_ref[hp])
        rk = r * kp * rk_ref[hp]
        halves = []
        for j in range(2):
            sl = slice(j * H, (j + 1) * H)
            oj = out[:, sl]
            mean = jnp.mean(oj, axis=-1, keepdims=True)
            cj = oj - mean
            var = jnp.mean(cj * cj, axis=-1, keepdims=True)
            nj = cj * lax.rsqrt(var + RW_GN_EPS)
            bonus = jnp.sum(rk[:, sl], axis=-1, keepdims=True) * v[:, sl]
            halves.append((nj, bonus))
        normed = jnp.concatenate([halves[0][0], halves[1][0]], axis=1)
        bonus = jnp.concatenate([halves[0][1], halves[1][1]], axis=1)
        lanes = slice(hp * LANES, (hp + 1) * LANES)
        o_ref[0, :, lanes] = (normed * gg_ref[hp] + gb_ref[hp] + bonus) * g_ref[0, :, lanes]


def _rw_post(out_f, out_b, r, k, v, a, g, k_a, r_k, gn_g, gn_b):
    B, _, T, _ = r.shape
    tt = min(256, T)
    hs = pl.BlockSpec((1, RW_PAIRS, tt, LANES), lambda b, i: (b, 0, i, 0))
    ps = pl.BlockSpec((RW_PAIRS, 1, LANES), lambda b, i: (0, 0, 0))
    ts = pl.BlockSpec((1, tt, RW_W), lambda b, i: (b, i, 0))
    return pl.pallas_call(
        _rw_post_body,
        grid=(B, T // tt),
        in_specs=[hs] * 6 + [ts] + [ps] * 4,
        out_specs=ts,
        out_shape=jax.ShapeDtypeStruct((B, T, RW_W), f32),
        compiler_params=_cparams(("parallel", "parallel")),
        name="rw_post",
    )(out_f, out_b, r, k, v, a, g, k_a, r_k, gn_g, gn_b)


def _merge_body(hf_ref, hb_ref, o_ref, ng_ref, rw_ref, ga_ref, gb_ref, x_ref, wa_ref, wb_ref, wo_ref, nf_ref,
                wr_ref, br_ref, x1_ref, xn_ref, lg_ref):
    heads = []
    for h in range(ML_HEADS):
        sl = slice(h * ML_DV, (h + 1) * ML_DV)
        t = hf_ref[:, sl] + hb_ref[:, sl]
        t = t * lax.rsqrt(jnp.mean(t * t, axis=-1, keepdims=True) + NORM_EPS)
        heads.append((_sigmoid(o_ref[:, sl]) * (t * ng_ref[:, sl])).astype(bf16))
    ml = jnp.concatenate(heads, axis=1)
    y_a = jnp.dot(ml, wa_ref[...], preferred_element_type=f32)
    y_b = jnp.dot(rw_ref[...].astype(bf16), wb_ref[...], preferred_element_type=f32)
    merged = _sigmoid(ga_ref[...]) * y_a + _sigmoid(gb_ref[...]) * y_b
    x1 = x_ref[...] + jnp.dot(merged.astype(bf16), wo_ref[...], preferred_element_type=f32)
    x1_ref[...] = x1
    xn = x1 * lax.rsqrt(jnp.mean(x1 * x1, axis=-1, keepdims=True) + NORM_EPS) * nf_ref[...]
    xn_ref[...] = xn
    lg_ref[...] = jnp.dot(xn, wr_ref[...], precision=lax.Precision.HIGHEST,
                          preferred_element_type=f32) + br_ref[...]


def _merge(h_f, h_b, ml_norm_g, rw_out, proj, x2d, w_up_a, w_up_b, w_out, norm_ffn, w_router, b_router):
    n = x2d.shape[0]
    tm = min(512, n)
    row = lambda blk: pl.BlockSpec((tm, D_MODEL), lambda i: (i, blk))
    full = lambda shape: pl.BlockSpec(shape, lambda i: (0,) * len(shape))
    return pl.pallas_call(
        _merge_body,
        grid=(n // tm,),
        in_specs=[row(0), row(0), row(2), full((1, ML_V_W)), row(0), row(6), row(7), row(0),
                  full((D_MODEL, D_MODEL)), full((D_MODEL, D_MODEL)), full((D_MODEL, D_MODEL)),
                  full((1, D_MODEL)), full((D_MODEL, LANES)), full((1, LANES))],
        out_specs=[row(0), row(0), pl.BlockSpec((tm, LANES), lambda i: (i, 0))],
        out_shape=[jax.ShapeDtypeStruct((n, D_MODEL), f32), jax.ShapeDtypeStruct((n, D_MODEL), f32),
                   jax.ShapeDtypeStruct((n, LANES), f32)],
        compiler_params=_cparams(("parallel",)),
        name="merge",
    )(h_f, h_b, proj, ml_norm_g, rw_out, proj, proj, x2d, w_up_a, w_up_b, w_out, norm_ffn, w_router, b_router)


def _first_argmax(p, width):
    lane = lax.broadcasted_iota(i32, p.shape, 1)
    top = jnp.max(p, axis=-1, keepdims=True)
    idx = jnp.min(jnp.where(p == top, lane, width), axis=-1, keepdims=True)
    return top, idx, lane


def _route_body(lg_ref, meta_ref, gate_ref, cnt_ref, carry_ref):
    @pl.when(pl.program_id(0) == 0)
    def _():
        carry_ref[...] = jnp.zeros_like(carry_ref)

    lg = lg_ref[...]
    tr = lg.shape[0]
    gl = lg[:, 0:N_GROUPS]
    ge = jnp.exp(gl - jnp.max(gl, axis=-1, keepdims=True))
    p_group = ge / jnp.sum(ge, axis=-1, keepdims=True)
    p_g, g_idx, _ = _first_argmax(p_group, N_GROUPS)
    el = jnp.zeros((tr, EXPERTS_PER_GROUP), f32)
    for g in range(N_GROUPS):
        lo = N_GROUPS + g * EXPERTS_PER_GROUP
        el = el + jnp.where(g_idx == g, 1.0, 0.0) * lg[:, lo:lo + EXPERTS_PER_GROUP]
    ee = jnp.exp(el - jnp.max(el, axis=-1, keepdims=True))
    p_exp = ee / jnp.sum(ee, axis=-1, keepdims=True)
    p1, i1, lane8 = _first_argmax(p_exp, EXPERTS_PER_GROUP)
    p2, i2, _ = _first_argmax(jnp.where(lane8 == i1, -1.0, p_exp), EXPERTS_PER_GROUP)
    psum = p1 + p2
    g1 = p_g * p1 / psum
    g2 = p_g * p2 / psum
    e0 = g_idx * EXPERTS_PER_GROUP + i1
    e1 = g_idx * EXPERTS_PER_GROUP + i2
    lane = lax.broadcasted_iota(i32, (tr, LANES), 1)
    hit0 = lane == e0
    hit1 = lane == e1
    onehot = jnp.where(hit0 | hit1, 1.0, 0.0)
    rowi = lax.broadcasted_iota(i32, (tr, tr), 0)
    coli = lax.broadcasted_iota(i32, (tr, tr), 1)
    before = jnp.where(coli < rowi, 1.0, 0.0)
    prefix = _dot(before, onehot) + carry_ref[...]
    r0 = jnp.sum(jnp.where(hit0, prefix, 0.0), axis=-1, keepdims=True).astype(i32)
    r1 = jnp.sum(jnp.where(hit1, prefix, 0.0), axis=-1, keepdims=True).astype(i32)
    carry = carry_ref[...] + jnp.sum(onehot, axis=0, keepdims=True)
    carry_ref[...] = carry
    cnt_ref[...] = carry.astype(i32)
    meta_ref[...] = jnp.where(lane == 0, e0, jnp.where(lane == 1, e1, jnp.where(lane == 2, r0, r1)))
    gate_ref[...] = jnp.where(lane == 0, g1, g2)


def _route(logits):
    n = logits.shape[0]
    tr = min(512, n)
    return pl.pallas_call(
        _route_body,
        grid=(n // tr,),
        in_specs=[pl.BlockSpec((tr, LANES), lambda i: (i, 0))],
        out_specs=[pl.BlockSpec((tr, LANES), lambda i: (i, 0)), pl.BlockSpec((tr, LANES), lambda i: (i, 0)),
                   pl.BlockSpec((1, LANES), lambda i: (0, 0))],
        out_shape=[jax.ShapeDtypeStruct((n, LANES), i32), jax.ShapeDtypeStruct((n, LANES), f32),
                   jax.ShapeDtypeStruct((1, LANES), i32)],
        scratch_shapes=[pltpu.VMEM((1, LANES), f32)],
        compiler_params=_cparams(("arbitrary",)),
        name="route",
    )(logits)


def _dest_body(meta_ref, ps_ref, o_ref):
    meta = meta_ref[...]
    lane = lax.broadcasted_iota(i32, meta.shape, 1)
    ps = ps_ref[...].astype(f32)
    seg0 = jnp.sum(jnp.where(lane == meta[:, 0:1], ps, 0.0), axis=-1, keepdims=True).astype(i32)
    seg1 = jnp.sum(jnp.where(lane == meta[:, 1:2], ps, 0.0), axis=-1, keepdims=True).astype(i32)
    o_ref[...] = jnp.where(lane == 0, seg0 + meta[:, 2:3], seg1 + meta[:, 3:4])


def _dest(meta, pstart_row):
    n = meta.shape[0]
    tr = min(1024, n)
    return pl.pallas_call(
        _dest_body,
        grid=(n // tr,),
        in_specs=[pl.BlockSpec((tr, LANES), lambda i: (i, 0)), pl.BlockSpec((1, LANES), lambda i: (0, 0))],
        out_specs=pl.BlockSpec((tr, LANES), lambda i: (i, 0)),
        out_shape=jax.ShapeDtypeStruct((n, LANES), i32),
        compiler_params=_cparams(("parallel",)),
        name="moe_dest",
    )(meta, pstart_row)


def _scatter_body(d0_ref, d1_ref, x_ref, buf_in, xs_hbm, sem):
    del buf_in
    ts = d0_ref.shape[0]

    def row_copy(t, dst):
        return pltpu.make_async_copy(x_ref.at[pl.ds(t, 1)], xs_hbm.at[pl.ds(dst, 1)], sem)

    def issue(t, c):
        row_copy(t, d0_ref[t]).start()
        row_copy(t, d1_ref[t]).start()
        return c

    lax.fori_loop(0, ts, issue, 0, unroll=DMA_UNROLL)

    def drain(t, c):
        row_copy(0, 0).wait()
        row_copy(0, 0).wait()
        return c

    lax.fori_loop(0, ts, drain, 0, unroll=DMA_UNROLL)


def _scatter(xn, d0, d1, n_rows):
    n = xn.shape[0]
    ts = min(512, n)
    tok = pl.BlockSpec((ts,), lambda i: (i,), memory_space=pltpu.SMEM)
    anyspec = pl.BlockSpec(memory_space=pl.ANY)
    buf = jnp.zeros((n_rows, D_MODEL), f32)
    return pl.pallas_call(
        _scatter_body,
        grid=(n // ts,),
        in_specs=[tok, tok, pl.BlockSpec((ts, D_MODEL), lambda i: (i, 0)), anyspec],
        out_specs=anyspec,
        scratch_shapes=[pltpu.SemaphoreType.DMA(())],
        out_shape=jax.ShapeDtypeStruct((n_rows, D_MODEL), f32),
        input_output_aliases={3: 0},
        compiler_params=_cparams(("arbitrary",)),
        name="moe_scatter",
    )(d0, d1, xn, buf)


def _experts_body(be_ref, nu_ref, xs_ref, wg_ref, wu_ref, wd_ref, ys_ref):
    del be_ref

    @pl.when(pl.program_id(0) < nu_ref[0])
    def _():
        xb = xs_ref[...].astype(bf16)
        hg = jnp.dot(xb, wg_ref[0], preferred_element_type=f32)
        hu = jnp.dot(xb, wu_ref[0], preferred_element_type=f32)
        hb = hg * _sigmoid(hg) * hu
        ys_ref[...] = jnp.dot(hb.astype(bf16), wd_ref[0], preferred_element_type=f32)

    @pl.when(pl.program_id(0) >= nu_ref[0])
    def _():
        ys_ref[...] = jnp.zeros_like(ys_ref)


def _experts(xs, block_e, n_used, w_gate, w_up, w_down):
    n_rows = xs.shape[0]
    nb = n_rows // MOE_BLOCK
    return pl.pallas_call(
        _experts_body,
        grid_spec=pltpu.PrefetchScalarGridSpec(
            num_scalar_prefetch=2, grid=(nb,),
            in_specs=[pl.BlockSpec((MOE_BLOCK, D_MODEL), lambda i, be, nu: (i, 0)),
                      pl.BlockSpec((1, D_MODEL, D_EXPERT), lambda i, be, nu: (be[i], 0, 0)),
                      pl.BlockSpec((1, D_MODEL, D_EXPERT), lambda i, be, nu: (be[i], 0, 0)),
                      pl.BlockSpec((1, D_EXPERT, D_MODEL), lambda i, be, nu: (be[i], 0, 0))],
            out_specs=pl.BlockSpec((MOE_BLOCK, D_MODEL), lambda i, be, nu: (i, 0))),
        out_shape=jax.ShapeDtypeStruct((n_rows, D_MODEL), f32),
        compiler_params=_cparams(("arbitrary",)),
        name="moe_experts",
    )(block_e, n_used, xs, w_gate, w_up, w_down)


def _combine_body(d0_ref, d1_ref, x1_ref, gate_ref, nf_ref, ys_hbm, o_ref, y0_ref, y1_ref, sem):
    tc = d0_ref.shape[0]

    def row_copy(src, dst_ref, t):
        return pltpu.make_async_copy(ys_hbm.at[pl.ds(src, 1)], dst_ref.at[pl.ds(t, 1)], sem)

    def issue(t, c):
        row_copy(d0_ref[t], y0_ref, t).start()
        row_copy(d1_ref[t], y1_ref, t).start()
        return c

    lax.fori_loop(0, tc, issue, 0, unroll=DMA_UNROLL)

    def drain(t, c):
        row_copy(0, y0_ref, 0).wait()
        row_copy(0, y1_ref, 0).wait()
        return c

    lax.fori_loop(0, tc, drain, 0, unroll=DMA_UNROLL)
    gates = gate_ref[...]
    x2 = x1_ref[...] + gates[:, 0:1] * y0_ref[...] + gates[:, 1:2] * y1_ref[...]
    o_ref[...] = x2 * lax.rsqrt(jnp.mean(x2 * x2, axis=-1, keepdims=True) + NORM_EPS) * nf_ref[...]


def _combine(x1, gates, norm_final, ys, d0, d1):
    n = x1.shape[0]
    tc = min(256, n)
    tok = pl.BlockSpec((tc,), lambda i: (i,), memory_space=pltpu.SMEM)
    return pl.pallas_call(
        _combine_body,
        grid=(n // tc,),
        in_specs=[tok, tok,
                  pl.BlockSpec((tc, D_MODEL), lambda i: (i, 0)),
                  pl.BlockSpec((tc, LANES), lambda i: (i, 0)),
                  pl.BlockSpec((1, D_MODEL), lambda i: (0, 0)),
                  pl.BlockSpec(memory_space=pl.ANY)],
        out_specs=pl.BlockSpec((tc, D_MODEL), lambda i: (i, 0)),
        scratch_shapes=[pltpu.VMEM((tc, D_MODEL), f32), pltpu.VMEM((tc, D_MODEL), f32),
                        pltpu.SemaphoreType.DMA(())],
        out_shape=jax.ShapeDtypeStruct((n, D_MODEL), f32),
        compiler_params=_cparams(("arbitrary",)),
        name="moe_combine",
    )(d0, d1, x1, gates, norm_final, ys)


def _prep_weights(w_in, rw_mu, w_router_group, b_router_group, w_router_expert, b_router_expert):
    rw0 = ML_COLS
    g0 = ML_COLS + RW_COLS
    zeros = lambda c: jnp.zeros((D_MODEL, c), w_in.dtype)
    w_all = jnp.concatenate([
        w_in[:, 0:3072],
        w_in[:, rw0:rw0 + 3072],
        w_in[:, g0:g0 + 2048],
        w_in[:, 3072:3088], zeros(112),
        w_in[:, rw0 + 3072:rw0 + 3392], zeros(64)], axis=1).astype(bf16)
    mu_r = rw_mu[None, 0:1024]
    mu_k = rw_mu[None, 1024:2048]
    mu_v = rw_mu[None, 2048:3072]
    mu_l = jnp.concatenate([jnp.zeros((128,), f32), rw_mu[3072:3392], jnp.zeros((64,), f32)])[None]
    w_router = jnp.concatenate([w_router_group, w_router_expert,
                                jnp.zeros((D_MODEL, LANES - N_GROUPS - N_EXPERTS), f32)], axis=1)
    b_router = jnp.concatenate([b_router_group, b_router_expert,
                                jnp.zeros((LANES - N_GROUPS - N_EXPERTS,), f32)])[None]
    return w_all, mu_r, mu_k, mu_v, mu_l, w_router, b_router


def _pairs(p):
    return p.reshape(RW_PAIRS, 1, LANES)


def _forward(x, norm_mix, w_all, mus, ml_b_i, ml_b_f, ml_norm_g, rw_w0, rw_w2, rw_a0, rw_a2, rw_g2,
             rw_k_k, rw_k_a, rw_r_k, rw_gn_g, rw_gn_b, w_up_a, w_up_b, w_out, norm_ffn,
             w_router, b_router, w_gate, w_up, w_down, norm_final):
    B, T, _ = x.shape
    n = B * T
    x2d = x.reshape(n, D_MODEL)
    proj = _proj(x2d, norm_mix[None], w_all)
    proj3 = proj.reshape(B, T, W_ALL)
    gates_t = jnp.swapaxes(proj3[:, :, 8 * D_MODEL:8 * D_MODEL + 16], 1, 2)
    bias16 = jnp.concatenate([ml_b_i.reshape(-1), ml_b_f.reshape(-1)])
    bias_row = jnp.concatenate([bias16, jnp.zeros((LANES - 16,), f32)])[None]
    h_f, h_b = _mlstm(proj3, gates_t, bias_row, bias16[:, None])
    mu_r, mu_k, mu_v, mu_l = mus
    r, k, v, a, lwf, lwb, g = _rw_prep(proj3, mu_r, mu_k, mu_v, mu_l, rw_w0, rw_w2, rw_a0[None], rw_a2, rw_g2)
    out_f, out_b = _rw_scan(r, k, v, a, lwf, lwb, _pairs(rw_k_k), _pairs(rw_k_a))
    rw_out = _rw_post(out_f, out_b, r, k, v, a, g, _pairs(rw_k_a), _pairs(rw_r_k), _pairs(rw_gn_g),
                      _pairs(rw_gn_b))
    x1, xn, logits = _merge(h_f.reshape(n, ML_V_W), h_b.reshape(n, ML_V_W), ml_norm_g[None],
                            rw_out.reshape(n, RW_W), proj, x2d,
                            w_up_a, w_up_b, w_out, norm_ffn[None], w_router, b_router)
    meta, gates, counts = _route(logits)
    counts = counts[0, :N_EXPERTS]
    padded = ((counts + MOE_BLOCK - 1) // MOE_BLOCK) * MOE_BLOCK
    pend = jnp.cumsum(padded)
    pstart = (pend - padded).astype(i32)
    nb = (2 * n) // MOE_BLOCK + N_EXPERTS
    block_start = jnp.arange(nb, dtype=i32) * MOE_BLOCK
    block_e = jnp.minimum(jnp.sum(block_start[:, None] >= pend[None, :], axis=1), N_EXPERTS - 1).astype(i32)
    n_used = (pend[-1:] // MOE_BLOCK).astype(i32)
    pstart_row = jnp.concatenate([pstart, jnp.zeros((LANES - N_EXPERTS,), i32)])[None]
    dest = _dest(meta, pstart_row)
    d0, d1 = dest[:, 0], dest[:, 1]
    xs = _scatter(xn, d0, d1, nb * MOE_BLOCK)
    ys = _experts(xs, block_e, n_used, w_gate, w_up, w_down)
    y = _combine(x1, gates, norm_final[None], ys, d0, d1)
    return y.reshape(B, T, D_MODEL)


def kernel(x_prompt, x_sample, norm_mix, w_in, ml_b_i, ml_b_f, ml_norm_g, rw_mu, rw_w0, rw_w2, rw_a0, rw_a2,
           rw_g2, rw_k_k, rw_k_a, rw_r_k, rw_gn_g, rw_gn_b, w_up_a, w_up_b, w_out, norm_ffn,
           w_router_group, b_router_group, w_router_expert, b_router_expert,
           w_expert_gate, w_expert_up, w_expert_down, norm_final):
    w_all, mu_r, mu_k, mu_v, mu_l, w_router, b_router = _prep_weights(
        w_in[0], rw_mu[0], w_router_group[0], b_router_group[0], w_router_expert[0], b_router_expert[0])
    args = (norm_mix[0], w_all, (mu_r, mu_k, mu_v, mu_l), ml_b_i[0], ml_b_f[0], ml_norm_g[0],
            rw_w0[0], rw_w2[0], rw_a0[0], rw_a2[0], rw_g2[0], rw_k_k[0], rw_k_a[0], rw_r_k[0].reshape(-1),
            rw_gn_g[0], rw_gn_b[0], w_up_a[0].astype(bf16), w_up_b[0].astype(bf16), w_out[0].astype(bf16),
            norm_ffn[0], w_router, b_router, w_expert_gate[0].astype(bf16), w_expert_up[0].astype(bf16),
            w_expert_down[0].astype(bf16), norm_final)
    return (_forward(x_prompt, *args), _forward(x_sample, *args))
```

```python
import functools

import jax
import jax.numpy as jnp
from jax import lax
from jax.experimental import pallas as pl
from jax.experimental.pallas import tpu as pltpu

f32 = jnp.float32
bf16 = jnp.bfloat16
i32 = jnp.int32

D_MODEL = 1024
ML_HEADS = 4
ML_DQK = 128
ML_DV = 256
ML_QK_W = ML_HEADS * ML_DQK
ML_V_W = ML_HEADS * ML_DV
ML_COLS = 2 * ML_QK_W + 2 * ML_V_W + 4 * ML_HEADS
RW_HEAD = 64
RW_HEADS = 16
RW_W = 1024
RW_PAIRS = RW_HEADS // 2
RW_COLS = 3 * RW_W + 64 + 64 + 64 + 128
N_GROUPS = 8
EXPERTS_PER_GROUP = 8
N_EXPERTS = 64
D_EXPERT = 512
NORM_EPS = 1e-6
RW_GN_EPS = 64e-5
L2_EPS = 1e-12

LANES = 128
SUBLANES = 8
W_ALL = 8 * D_MODEL + 512
SMALL_BLK = (8 * D_MODEL) // 512
ML_CHUNK = 128
RW_CHUNK = 64
RW_GROUP = 8
MOE_BLOCK = 256
DMA_UNROLL = 8
VMEM_LIMIT = 56 * 1024 * 1024


def _cparams(sem):
    return pltpu.CompilerParams(dimension_semantics=sem, vmem_limit_bytes=VMEM_LIMIT)


def _dot(a, b):
    return jnp.dot(a.astype(bf16), b.astype(bf16), preferred_element_type=f32)


def _dot_nt(a, b):
    return lax.dot_general(a.astype(bf16), b.astype(bf16), (((1,), (1,)), ((), ())),
                           preferred_element_type=f32)


def _dot_tn(a, b):
    return lax.dot_general(a.astype(bf16), b.astype(bf16), (((0,), (0,)), ((), ())),
                           preferred_element_type=f32)


def _sigmoid(x):
    return 1.0 / (1.0 + jnp.exp(-x))


def _softplus(x):
    return jnp.maximum(x, 0.0) + jnp.log1p(jnp.exp(-jnp.abs(x)))


def _proj_body(x_ref, g_ref, w_ref, o_ref, xn_ref):
    @pl.when(pl.program_id(1) == 0)
    def _():
        x = x_ref[...]
        ms = jnp.mean(x * x, axis=-1, keepdims=True)
        xn_ref[...] = (x * lax.rsqrt(ms + NORM_EPS) * g_ref[...]).astype(bf16)

    o_ref[...] = jnp.dot(xn_ref[...], w_ref[...], preferred_element_type=f32)


def _proj(x2d, g, w_all):
    n = x2d.shape[0]
    tm = min(1024, n)
    tn = W_ALL // 4
    return pl.pallas_call(
        _proj_body,
        grid=(n // tm, W_ALL // tn),
        in_specs=[pl.BlockSpec((tm, D_MODEL), lambda i, j: (i, 0)),
                  pl.BlockSpec((1, D_MODEL), lambda i, j: (0, 0)),
                  pl.BlockSpec((D_MODEL, tn), lambda i, j: (0, j))],
        out_specs=pl.BlockSpec((tm, tn), lambda i, j: (i, j)),
        out_shape=jax.ShapeDtypeStruct((n, W_ALL), f32),
        scratch_shapes=[pltpu.VMEM((tm, D_MODEL), bf16)],
        compiler_params=_cparams(("parallel", "arbitrary")),
        name="proj",
    )(x2d, g, w_all)


def _log_sigmoid(x):
    return jnp.minimum(x, 0.0) - jnp.log1p(jnp.exp(-jnp.abs(x)))


def _mlstm_body(br_ref, bc_ref, qf_ref, kf_ref, vf_ref, gf_ref, gtf_ref,
                qb_ref, kb_ref, vb_ref, gb_ref, gtb_ref, hf_ref, hb_ref, c_ref, n_ref, m_ref):
    L = ML_CHUNK

    @pl.when(pl.program_id(1) == 0)
    def _():
        c_ref[...] = jnp.zeros_like(c_ref)
        n_ref[...] = jnp.zeros_like(n_ref)
        m_ref[...] = jnp.zeros_like(m_ref)

    rowi = lax.broadcasted_iota(i32, (L, L), 0)
    coli = lax.broadcasted_iota(i32, (L, L), 1)
    hp = lax.Precision.HIGHEST
    scale = ML_DQK ** -0.5
    units = []
    for d, (q_ref, k_ref, v_ref, g_ref, gt_ref, out_ref) in enumerate(
            ((qf_ref, kf_ref, vf_ref, gf_ref, gtf_ref, hf_ref), (qb_ref, kb_ref, vb_ref, gb_ref, gtb_ref, hb_ref))):
        mask = (coli >= rowi) if d else (coli <= rowi)
        tri = jnp.where(mask, 1.0, 0.0)
        g_all = g_ref[0] + br_ref[...]
        gt_all = gt_ref[0] + bc_ref[...]
        bcol_all = jnp.dot(tri, _log_sigmoid(g_all), precision=hp, preferred_element_type=f32)
        brow_all = lax.dot_general(_log_sigmoid(gt_all), tri, (((1,), (1,)), ((), ())), precision=hp,
                                   preferred_element_type=f32)
        for h in range(ML_HEADS):
            ii = 4 * d + h
            fi = 8 + 4 * d + h
            units.append(dict(
                d=d, h=h, mask=mask, out_ref=out_ref,
                i_col=g_all[:, ii:ii + 1], i_row=gt_all[ii:ii + 1, :],
                b_col=bcol_all[:, fi:fi + 1], b_row=brow_all[fi:fi + 1, :],
                q=q_ref[0, :, h * ML_DQK:(h + 1) * ML_DQK] * scale,
                k=k_ref[0, :, h * ML_DQK:(h + 1) * ML_DQK],
                v=v_ref[0, :, h * ML_DV:(h + 1) * ML_DV],
                c=c_ref[d, h], n=n_ref[d, h], m=m_ref[d, h]))
    for u in units:
        logw = jnp.where(u['mask'], u['b_col'] - u['b_row'] + u['i_row'], -jnp.inf)
        m_inter = u['b_col'] + u['m']
        u['m_t'] = jnp.maximum(m_inter, jnp.max(logw, axis=1, keepdims=True))
        u['dm'] = jnp.exp(logw - u['m_t'])
        u['inter'] = jnp.exp(m_inter - u['m_t'])
    qk = [_dot_nt(u['q'], u['k']) for u in units]
    s = [qk[i] * u['dm'] for i, u in enumerate(units)]
    sv = [_dot(s[i], u['v']) for i, u in enumerate(units)]
    qc = [_dot(u['q'], u['c']) for u in units]
    for i, u in enumerate(units):
        den = (jnp.sum(s[i], axis=1, keepdims=True)
               + u['inter'] * jnp.sum(u['q'] * u['n'], axis=1, keepdims=True))
        hh = (sv[i] + u['inter'] * qc[i]) / jnp.maximum(jnp.abs(den), jnp.exp(-u['m_t']))
        u['out_ref'][0, :, u['h'] * ML_DV:(u['h'] + 1) * ML_DV] = hh
        b_last = u['b_col'][0:1] if u['d'] else u['b_col'][L - 1:L]
        logu_col = b_last - u['b_col'] + u['i_col']
        logu_row = b_last - u['b_row'] + u['i_row']
        u['m_new'] = jnp.maximum(b_last + u['m'], jnp.max(logu_row, axis=1, keepdims=True))
        u['decay'] = jnp.exp(b_last + u['m'] - u['m_new'])
        u['ku'] = u['k'] * jnp.exp(logu_col - u['m_new'])
    kv = [_dot_tn(u['ku'], u['v']) for u in units]
    for i, u in enumerate(units):
        d, h = u['d'], u['h']
        c_ref[d, h] = u['decay'] * u['c'] + kv[i]
        n_ref[d, h] = u['decay'] * u['n'] + jnp.sum(u['ku'], axis=0, keepdims=True)
        m_ref[d, h] = u['m_new']


def _mlstm(proj3, gates_t, bias_row, bias_col):
    B, T, _ = proj3.shape
    L = ML_CHUNK
    nc = T // L

    def specs(cm):
        return [pl.BlockSpec((1, L, ML_QK_W), lambda b, c: (b, cm(c), 0)),
                pl.BlockSpec((1, L, ML_QK_W), lambda b, c: (b, cm(c), 1)),
                pl.BlockSpec((1, L, ML_V_W), lambda b, c: (b, cm(c), 1)),
                pl.BlockSpec((1, L, LANES), lambda b, c: (b, cm(c), (8 * D_MODEL) // LANES)),
                pl.BlockSpec((1, 16, L), lambda b, c: (b, 0, cm(c)))]

    fwd = lambda c: c
    bwd = lambda c: nc - 1 - c
    out = jax.ShapeDtypeStruct((B, T, ML_V_W), f32)
    return pl.pallas_call(
        _mlstm_body,
        grid=(B, nc),
        in_specs=[pl.BlockSpec((1, LANES), lambda b, c: (0, 0)), pl.BlockSpec((16, 1), lambda b, c: (0, 0))]
        + specs(fwd) + specs(bwd),
        out_specs=[pl.BlockSpec((1, L, ML_V_W), lambda b, c: (b, c, 0)),
                   pl.BlockSpec((1, L, ML_V_W), lambda b, c: (b, bwd(c), 0))],
        out_shape=[out, out],
        scratch_shapes=[pltpu.VMEM((2, ML_HEADS, ML_DQK, ML_DV), f32),
                        pltpu.VMEM((2, ML_HEADS, 1, ML_DQK), f32),
                        pltpu.VMEM((2, ML_HEADS, 1, 1), f32)],
        compiler_params=_cparams(("parallel", "arbitrary")),
        name="mlstm",
    )(bias_row, bias_col, *([proj3] * 4), gates_t, *([proj3] * 4), gates_t)


def _shift(p, prev8, next8, mu, first, last):
    tt = p.shape[0]
    rows = lax.broadcasted_iota(i32, p.shape, 0)
    prev_row = jnp.where(first, 0.0, prev8[SUBLANES - 1:SUBLANES])
    next_row = jnp.where(last, 0.0, next8[0:1])
    pm = jnp.where(rows == 0, prev_row, pltpu.roll(p, 1, axis=0))
    nx = jnp.where(rows == tt - 1, next_row, pltpu.roll(p, tt - 1, axis=0))
    return p + mu * (0.5 * (pm + nx) - p)


def _rw_prep_body(r_ref, rp_ref, rn_ref, k_ref, kp_ref, kn_ref, v_ref, vp_ref, vn_ref,
                  l_ref, lp_ref, ln_ref, mur_ref, muk_ref, muv_ref, mul_ref,
                  w0_ref, w2h_ref, w2l_ref, a0_ref, a2_ref, g2_ref,
                  ro_ref, ko_ref, vo_ref, ao_ref, lwf_ref, lwb_ref, go_ref):
    i = pl.program_id(1)
    first = i == 0
    last = i == pl.num_programs(1) - 1
    lo = _shift(l_ref[0], lp_ref[0], ln_ref[0], mul_ref[...], first, last)
    ad = lo[:, 256:320]
    sg = _sigmoid(lo[:, 320:448])
    th = [jnp.tanh(lo[:, 128:192]), jnp.tanh(lo[:, 192:256])]
    th_hi = [t.astype(bf16) for t in th]
    th_lo = [(t - h.astype(f32)).astype(bf16) for t, h in zip(th, th_hi)]

    def mm(a, b):
        return jnp.dot(a, b, preferred_element_type=f32)

    CW = 2 * LANES
    for cb in range(RW_W // CW):
        sl = slice(cb * CW, (cb + 1) * CW)
        cols = []
        for src, prv, nxt, mu in ((r_ref, rp_ref, rn_ref, mur_ref), (k_ref, kp_ref, kn_ref, muk_ref),
                                  (v_ref, vp_ref, vn_ref, muv_ref)):
            cols.append(_shift(src[0, :, sl], prv[0, :, sl], nxt[0, :, sl], mu[:, sl], first, last))
        lws = []
        for d in range(2):
            wl = w0_ref[d:d + 1, sl] + (mm(th_hi[d], w2h_ref[d, :, sl]) + mm(th_hi[d], w2l_ref[d, :, sl])
                                        + mm(th_lo[d], w2h_ref[d, :, sl]))
            lws.append(-jnp.exp(-_softplus(-wl) - 0.5))
        a = _sigmoid(a0_ref[:, sl] + _dot(ad, a2_ref[:, sl]))
        go_ref[0, :, sl] = _dot(sg, g2_ref[:, sl])
        for j in range(2):
            hp = 2 * cb + j
            half = slice(j * LANES, (j + 1) * LANES)
            ro_ref[0, hp] = cols[0][:, half]
            ko_ref[0, hp] = cols[1][:, half]
            vo_ref[0, hp] = cols[2][:, half]
            ao_ref[0, hp] = a[:, half]
            lwf_ref[0, hp] = lws[0][:, half]
            lwb_ref[0, hp] = lws[1][:, half]


def _rw_prep(proj3, mu_r, mu_k, mu_v, mu_l, w0, w2h, w2l, a0, a2, g2):
    B, T, _ = proj3.shape
    tt = min(256, T)
    nt = T // tt
    g8 = tt // SUBLANES
    n8 = T // SUBLANES

    def tile(width, blk):
        return [pl.BlockSpec((1, tt, width), lambda b, i: (b, i, blk)),
                pl.BlockSpec((1, SUBLANES, width), lambda b, i: (b, jnp.maximum(i * g8 - 1, 0), blk)),
                pl.BlockSpec((1, SUBLANES, width), lambda b, i: (b, jnp.minimum((i + 1) * g8, n8 - 1), blk))]

    full = lambda shape: pl.BlockSpec(shape, lambda b, i: (0,) * len(shape))
    in_specs = (tile(D_MODEL, 3) + tile(D_MODEL, 4) + tile(D_MODEL, 5) + tile(512, SMALL_BLK)
                + [full((1, RW_W))] * 3 + [full((1, 512)), full((2, RW_W)), full((2, 64, RW_W)), full((2, 64, RW_W)),
                                           full((1, RW_W)), full((64, RW_W)), full((128, RW_W))])
    hm = jax.ShapeDtypeStruct((B, RW_PAIRS, T, LANES), f32)
    hm_spec = pl.BlockSpec((1, RW_PAIRS, tt, LANES), lambda b, i: (b, 0, i, 0))
    return pl.pallas_call(
        _rw_prep_body,
        grid=(B, nt),
        in_specs=in_specs,
        out_specs=[hm_spec] * 6 + [pl.BlockSpec((1, tt, RW_W), lambda b, i: (b, i, 0))],
        out_shape=[hm] * 6 + [jax.ShapeDtypeStruct((B, T, RW_W), f32)],
        compiler_params=_cparams(("parallel", "parallel")),
        name="rw_prep",
    )(*([proj3] * 12), mu_r, mu_k, mu_v, mu_l, w0, w2h, w2l, a0, a2, g2)


def _chunk_cumsum(x, reverse):
    L = x.shape[0]
    rows = lax.broadcasted_iota(i32, x.shape, 0)
    s = 1
    while s < L:
        if reverse:
            x = x + jnp.where(rows < L - s, pltpu.roll(x, L - s, axis=0), 0.0)
        else:
            x = x + jnp.where(rows >= s, pltpu.roll(x, s, axis=0), 0.0)
        s *= 2
    return x


def _block_diag(x, lo):
    return jnp.concatenate([jnp.where(lo, x, 0.0), jnp.where(lo, 0.0, x)], axis=0)


def _rw_units(units, rowi, coli, lo):
    L = units[0][0].shape[0]
    n = len(units)
    lhs, rhs, tail, scale = [], [], [], []
    for r, kp, v, kk, bh, lw, cum, S, reverse in units:
        tot = cum[0:1] if reverse else cum[L - 1:L]
        gi = jnp.exp(-cum)
        gl = jnp.exp(tot - cum)
        lhs.append(jnp.concatenate([-kk * jnp.exp(cum - lw), r * jnp.exp(cum)], axis=0))
        bt = bh * gi
        kt = kp * gi
        rhs.append(jnp.concatenate([_block_diag(bt, lo), _block_diag(kt, lo)], axis=0))
        tail.append(jnp.concatenate([bh * gl, kp * gl], axis=0))
        scale.append(jnp.exp(tot))
    G = [_dot_nt(lhs[i], rhs[i]) for i in range(n)]
    sv = [_dot_nt(lhs[i], units[i][7]) for i in range(n)]
    p, a_rb, low = [], [], []
    for i in range(n):
        reverse = units[i][8]
        strict = (coli > rowi) if reverse else (coli < rowi)
        incl = (coli >= rowi) if reverse else (coli <= rowi)
        p.append(jnp.where(strict, G[i][:L, :2 * L], 0.0))
        a_rb.append(jnp.where(incl, G[i][L:, :2 * L], 0.0))
        low.append(jnp.concatenate([jnp.where(strict, G[i][:L, 2 * L:], 0.0),
                                    jnp.where(incl, G[i][L:, 2 * L:], 0.0)], axis=0))
    av = [_dot(low[i], _block_diag(units[i][2], lo)) for i in range(n)]
    y = [sv[i][:L] + av[i][:L] for i in range(n)]
    s = 1
    while True:
        y = [y[i] + _dot(p[i], _block_diag(y[i], lo)) for i in range(n)]
        s *= 2
        if s >= L:
            break
        p = [_dot(p[i], _block_diag(p[i], lo)) for i in range(n)]
    ru = [_dot(a_rb[i], _block_diag(y[i], lo)) for i in range(n)]
    upd = [_dot_tn(jnp.concatenate([y[i], units[i][2]], axis=0), tail[i]) for i in range(n)]
    outs = [sv[i][L:] + av[i][L:] + ru[i] for i in range(n)]
    states = [units[i][7] * scale[i] + upd[i] for i in range(n)]
    return outs, states


def _rw_scan_body(rf_ref, kf_ref, vf_ref, af_ref, lwf_ref, rb_ref, kb_ref, vb_ref, ab_ref, lwb_ref,
                  kk_ref, ka_ref, of_ref, ob_ref, s_ref):
    L = RW_CHUNK
    H = RW_HEAD

    @pl.when(pl.program_id(1) == 0)
    def _():
        s_ref[...] = jnp.zeros_like(s_ref)

    rowi = lax.broadcasted_iota(i32, (L, 2 * L), 0)
    coli = lax.broadcasted_iota(i32, (L, 2 * L), 1) % L
    lo = lax.broadcasted_iota(i32, (1, LANES), 1) < H
    diag = (lax.broadcasted_iota(i32, (LANES, LANES), 0) < H) == (lax.broadcasted_iota(i32, (LANES, LANES), 1) < H)
    dirs = ((rf_ref, kf_ref, vf_ref, af_ref, lwf_ref), (rb_ref, kb_ref, vb_ref, ab_ref, lwb_ref))

    def group(it, carry):
        units = []
        for q in range(RW_GROUP):
            hp = it * RW_GROUP + q
            k_k = kk_ref[hp]
            k_a = ka_ref[hp]
            for d, (r_ref, k_ref, v_ref, a_ref, lw_ref) in enumerate(dirs):
                r = r_ref[0, hp]
                k = k_ref[0, hp]
                v = v_ref[0, hp]
                a = a_ref[0, hp]
                lw = lw_ref[0, hp]
                cum = _chunk_cumsum(lw, d == 1)
                kk = k * k_k
                sq = kk * kk
                n0 = jnp.sum(jnp.where(lo, sq, 0.0), axis=-1, keepdims=True)
                n1 = jnp.sum(jnp.where(lo, 0.0, sq), axis=-1, keepdims=True)
                kk = kk / jnp.maximum(jnp.sqrt(jnp.where(lo, n0, n1)), L2_EPS)
                kp = k * (1.0 + (a - 1.0) * k_a)
                units.append((r, kp, v, kk, kk * a, lw, cum, s_ref[d, hp], d == 1))
        outs, states = _rw_units(units, rowi, coli, lo)
        u = 0
        for q in range(RW_GROUP):
            hp = it * RW_GROUP + q
            for d, o_ref in enumerate((of_ref, ob_ref)):
                o_ref[0, hp] = outs[u]
                s_ref[d, hp] = jnp.where(diag, states[u], 0.0)
                u += 1
        return carry

    lax.fori_loop(0, RW_PAIRS // RW_GROUP, group, 0)


def _rw_scan(r, k, v, a, lwf, lwb, k_k, k_a):
    B, _, T, _ = r.shape
    L = RW_CHUNK
    nc = T // L
    fs = pl.BlockSpec((1, RW_PAIRS, L, LANES), lambda b, c: (b, 0, c, 0))
    bs = pl.BlockSpec((1, RW_PAIRS, L, LANES), lambda b, c: (b, 0, nc - 1 - c, 0))
    ps = pl.BlockSpec((RW_PAIRS, 1, LANES), lambda b, c: (0, 0, 0))
    hm = jax.ShapeDtypeStruct((B, RW_PAIRS, T, LANES), f32)
    return pl.pallas_call(
        _rw_scan_body,
        grid=(B, nc),
        in_specs=[fs] * 5 + [bs] * 5 + [ps, ps],
        out_specs=[fs, bs],
        out_shape=[hm, hm],
        scratch_shapes=[pltpu.VMEM((2, RW_PAIRS, LANES, LANES), f32)],
        compiler_params=_cparams(("parallel", "arbitrary")),
        name="rw_scan",
    )(r, k, v, a, lwf, r, k, v, a, lwb, k_k, k_a)


def _head_mean(x, avg):
    hi = x.astype(bf16)
    lo = (x - hi.astype(f32)).astype(bf16)
    return (jnp.dot(hi, avg, preferred_element_type=f32) + jnp.dot(lo, avg, preferred_element_type=f32))


def _rw_post_body(of_ref, ob_ref, r_ref, k_ref, v_ref, a_ref, g_ref, ka_ref, rk_ref, gg_ref, gb_ref, o_ref):
    same_head = (lax.broadcasted_iota(i32, (LANES, LANES), 0) < RW_HEAD) == (
        lax.broadcasted_iota(i32, (LANES, LANES), 1) < RW_HEAD)
    avg = jnp.where(same_head, 1.0 / RW_HEAD, 0.0).astype(bf16)
    for hp in range(RW_PAIRS):
        out = of_ref[0, hp] + ob_ref[0, hp]
        kp = k_ref[0, hp] * (1.0 + (a_ref[0, hp] - 1.0) * ka_ref[hp])
        rk = r_ref[0, hp] * kp * rk_ref[hp]
        cen = out - _head_mean(out, avg)
        var = _head_mean(cen * cen, avg)
        normed = cen * lax.rsqrt(var + RW_GN_EPS)
        bonus = (_head_mean(rk, avg) * RW_HEAD) * v_ref[0, hp]
        lanes = slice(hp * LANES, (hp + 1) * LANES)
        o_ref[0, :, lanes] = (normed * gg_ref[hp] + gb_ref[hp] + bonus) * g_ref[0, :, lanes]


def _rw_post(out_f, out_b, r, k, v, a, g, k_a, r_k, gn_g, gn_b):
    B, _, T, _ = r.shape
    tt = min(256, T)
    hs = pl.BlockSpec((1, RW_PAIRS, tt, LANES), lambda b, i: (b, 0, i, 0))
    ps = pl.BlockSpec((RW_PAIRS, 1, LANES), lambda b, i: (0, 0, 0))
    ts = pl.BlockSpec((1, tt, RW_W), lambda b, i: (b, i, 0))
    return pl.pallas_call(
        _rw_post_body,
        grid=(B, T // tt),
        in_specs=[hs] * 6 + [ts] + [ps] * 4,
        out_specs=ts,
        out_shape=jax.ShapeDtypeStruct((B, T, RW_W), f32),
        compiler_params=_cparams(("parallel", "parallel")),
        name="rw_post",
    )(out_f, out_b, r, k, v, a, g, k_a, r_k, gn_g, gn_b)


def _merge_body(hf_ref, hb_ref, o_ref, ng_ref, rw_ref, ga_ref, gb_ref, x_ref, wa_ref, wb_ref, wo_ref, nf_ref,
                wr_ref, br_ref, x1_ref, xn_ref, lg_ref):
    heads = []
    for h in range(ML_HEADS):
        sl = slice(h * ML_DV, (h + 1) * ML_DV)
        t = hf_ref[:, sl] + hb_ref[:, sl]
        t = t * lax.rsqrt(jnp.mean(t * t, axis=-1, keepdims=True) + NORM_EPS)
        heads.append((_sigmoid(o_ref[:, sl]) * (t * ng_ref[:, sl])).astype(bf16))
    ml = jnp.concatenate(heads, axis=1)
    y_a = jnp.dot(ml, wa_ref[...], preferred_element_type=f32)
    y_b = jnp.dot(rw_ref[...].astype(bf16), wb_ref[...], preferred_element_type=f32)
    merged = _sigmoid(ga_ref[...]) * y_a + _sigmoid(gb_ref[...]) * y_b
    x1 = x_ref[...] + jnp.dot(merged.astype(bf16), wo_ref[...], preferred_element_type=f32)
    x1_ref[...] = x1
    xn = x1 * lax.rsqrt(jnp.mean(x1 * x1, axis=-1, keepdims=True) + NORM_EPS) * nf_ref[...]
    xn_ref[...] = xn
    lg_ref[...] = jnp.dot(xn, wr_ref[...], precision=lax.Precision.HIGHEST,
                          preferred_element_type=f32) + br_ref[...]


def _merge(h_f, h_b, ml_norm_g, rw_out, proj, x2d, w_up_a, w_up_b, w_out, norm_ffn, w_router, b_router):
    n = x2d.shape[0]
    tm = min(512, n)
    row = lambda blk: pl.BlockSpec((tm, D_MODEL), lambda i: (i, blk))
    full = lambda shape: pl.BlockSpec(shape, lambda i: (0,) * len(shape))
    return pl.pallas_call(
        _merge_body,
        grid=(n // tm,),
        in_specs=[row(0), row(0), row(2), full((1, ML_V_W)), row(0), row(6), row(7), row(0),
                  full((D_MODEL, D_MODEL)), full((D_MODEL, D_MODEL)), full((D_MODEL, D_MODEL)),
                  full((1, D_MODEL)), full((D_MODEL, LANES)), full((1, LANES))],
        out_specs=[row(0), row(0), pl.BlockSpec((tm, LANES), lambda i: (i, 0))],
        out_shape=[jax.ShapeDtypeStruct((n, D_MODEL), f32), jax.ShapeDtypeStruct((n, D_MODEL), f32),
                   jax.ShapeDtypeStruct((n, LANES), f32)],
        compiler_params=_cparams(("parallel",)),
        name="merge",
    )(h_f, h_b, proj, ml_norm_g, rw_out, proj, proj, x2d, w_up_a, w_up_b, w_out, norm_ffn, w_router, b_router)


def _first_argmax(p, width):
    lane = lax.broadcasted_iota(i32, p.shape, 1)
    top = jnp.max(p, axis=-1, keepdims=True)
    idx = jnp.min(jnp.where(p == top, lane, width), axis=-1, keepdims=True)
    return top, idx, lane


def _route_body(lg_ref, meta_ref, gate_ref, cnt_ref, carry_ref):
    @pl.when(pl.program_id(0) == 0)
    def _():
        carry_ref[...] = jnp.zeros_like(carry_ref)

    lg = lg_ref[...]
    tr = lg.shape[0]
    gl = lg[:, 0:N_GROUPS]
    ge = jnp.exp(gl - jnp.max(gl, axis=-1, keepdims=True))
    p_group = ge / jnp.sum(ge, axis=-1, keepdims=True)
    p_g, g_idx, _ = _first_argmax(p_group, N_GROUPS)
    el = jnp.zeros((tr, EXPERTS_PER_GROUP), f32)
    for g in range(N_GROUPS):
        lo = N_GROUPS + g * EXPERTS_PER_GROUP
        el = el + jnp.where(g_idx == g, 1.0, 0.0) * lg[:, lo:lo + EXPERTS_PER_GROUP]
    ee = jnp.exp(el - jnp.max(el, axis=-1, keepdims=True))
    p_exp = ee / jnp.sum(ee, axis=-1, keepdims=True)
    p1, i1, lane8 = _first_argmax(p_exp, EXPERTS_PER_GROUP)
    p2, i2, _ = _first_argmax(jnp.where(lane8 == i1, -1.0, p_exp), EXPERTS_PER_GROUP)
    psum = p1 + p2
    g1 = p_g * p1 / psum
    g2 = p_g * p2 / psum
    e0 = g_idx * EXPERTS_PER_GROUP + i1
    e1 = g_idx * EXPERTS_PER_GROUP + i2
    lane = lax.broadcasted_iota(i32, (tr, LANES), 1)
    hit0 = lane == e0
    hit1 = lane == e1
    onehot = jnp.where(hit0 | hit1, 1.0, 0.0)
    rowi = lax.broadcasted_iota(i32, (tr, tr), 0)
    coli = lax.broadcasted_iota(i32, (tr, tr), 1)
    before = jnp.where(coli < rowi, 1.0, 0.0)
    prefix = _dot(before, onehot) + carry_ref[...]
    r0 = jnp.sum(jnp.where(hit0, prefix, 0.0), axis=-1, keepdims=True).astype(i32)
    r1 = jnp.sum(jnp.where(hit1, prefix, 0.0), axis=-1, keepdims=True).astype(i32)
    carry = carry_ref[...] + jnp.sum(onehot, axis=0, keepdims=True)
    carry_ref[...] = carry
    cnt_ref[...] = carry.astype(i32)
    meta_ref[...] = jnp.where(lane == 0, e0, jnp.where(lane == 1, e1, jnp.where(lane == 2, r0, r1)))
    gate_ref[...] = jnp.where(lane == 0, g1, g2)


def _route(logits):
    n = logits.shape[0]
    tr = min(512, n)
    return pl.pallas_call(
        _route_body,
        grid=(n // tr,),
        in_specs=[pl.BlockSpec((tr, LANES), lambda i: (i, 0))],
        out_specs=[pl.BlockSpec((tr, LANES), lambda i: (i, 0)), pl.BlockSpec((tr, LANES), lambda i: (i, 0)),
                   pl.BlockSpec((1, LANES), lambda i: (0, 0))],
        out_shape=[jax.ShapeDtypeStruct((n, LANES), i32), jax.ShapeDtypeStruct((n, LANES), f32),
                   jax.ShapeDtypeStruct((1, LANES), i32)],
        scratch_shapes=[pltpu.VMEM((1, LANES), f32)],
        compiler_params=_cparams(("arbitrary",)),
        name="route",
    )(logits)


def _dest_body(meta_ref, ps_ref, o_ref):
    meta = meta_ref[...]
    lane = lax.broadcasted_iota(i32, meta.shape, 1)
    ps = ps_ref[...].astype(f32)
    seg0 = jnp.sum(jnp.where(lane == meta[:, 0:1], ps, 0.0), axis=-1, keepdims=True).astype(i32)
    seg1 = jnp.sum(jnp.where(lane == meta[:, 1:2], ps, 0.0), axis=-1, keepdims=True).astype(i32)
    o_ref[...] = jnp.where(lane == 0, seg0 + meta[:, 2:3], seg1 + meta[:, 3:4])


def _dest(meta, pstart_row):
    n = meta.shape[0]
    tr = min(1024, n)
    return pl.pallas_call(
        _dest_body,
        grid=(n // tr,),
        in_specs=[pl.BlockSpec((tr, LANES), lambda i: (i, 0)), pl.BlockSpec((1, LANES), lambda i: (0, 0))],
        out_specs=pl.BlockSpec((tr, LANES), lambda i: (i, 0)),
        out_shape=jax.ShapeDtypeStruct((n, LANES), i32),
        compiler_params=_cparams(("parallel",)),
        name="moe_dest",
    )(meta, pstart_row)


def _scatter_body(d0_ref, d1_ref, x_ref, buf_in, xs_hbm, sem):
    del buf_in
    ts = d0_ref.shape[0]

    def row_copy(t, dst):
        return pltpu.make_async_copy(x_ref.at[pl.ds(t, 1)], xs_hbm.at[pl.ds(dst, 1)], sem)

    def issue(t, c):
        row_copy(t, d0_ref[t]).start()
        row_copy(t, d1_ref[t]).start()
        return c

    lax.fori_loop(0, ts, issue, 0, unroll=DMA_UNROLL)

    def drain(t, c):
        row_copy(0, 0).wait()
        row_copy(0, 0).wait()
        return c

    lax.fori_loop(0, ts, drain, 0, unroll=DMA_UNROLL)


def _scatter(xn, d0, d1, n_rows):
    n = xn.shape[0]
    ts = min(512, n)
    tok = pl.BlockSpec((ts,), lambda i: (i,), memory_space=pltpu.SMEM)
    anyspec = pl.BlockSpec(memory_space=pl.ANY)
    buf = jnp.zeros((n_rows, D_MODEL), f32)
    return pl.pallas_call(
        _scatter_body,
        grid=(n // ts,),
        in_specs=[tok, tok, pl.BlockSpec((ts, D_MODEL), lambda i: (i, 0)), anyspec],
        out_specs=anyspec,
        scratch_shapes=[pltpu.SemaphoreType.DMA(())],
        out_shape=jax.ShapeDtypeStruct((n_rows, D_MODEL), f32),
        input_output_aliases={3: 0},
        compiler_params=_cparams(("arbitrary",)),
        name="moe_scatter",
    )(d0, d1, xn, buf)


def _experts_body(be_ref, nu_ref, xs_ref, wg_ref, wu_ref, wd_ref, ys_ref):
    del be_ref

    @pl.when(pl.program_id(0) < nu_ref[0])
    def _():
        xb = xs_ref[...].astype(bf16)
        hg = jnp.dot(xb, wg_ref[0], preferred_element_type=f32)
        hu = jnp.dot(xb, wu_ref[0], preferred_element_type=f32)
        hb = hg * _sigmoid(hg) * hu
        ys_ref[...] = jnp.dot(hb.astype(bf16), wd_ref[0], preferred_element_type=f32)

    @pl.when(pl.program_id(0) >= nu_ref[0])
    def _():
        ys_ref[...] = jnp.zeros_like(ys_ref)


def _experts(xs, block_e, n_used, w_gate, w_up, w_down):
    n_rows = xs.shape[0]
    nb = n_rows // MOE_BLOCK
    return pl.pallas_call(
        _experts_body,
        grid_spec=pltpu.PrefetchScalarGridSpec(
            num_scalar_prefetch=2, grid=(nb,),
            in_specs=[pl.BlockSpec((MOE_BLOCK, D_MODEL), lambda i, be, nu: (i, 0)),
                      pl.BlockSpec((1, D_MODEL, D_EXPERT), lambda i, be, nu: (be[i], 0, 0)),
                      pl.BlockSpec((1, D_MODEL, D_EXPERT), lambda i, be, nu: (be[i], 0, 0)),
                      pl.BlockSpec((1, D_EXPERT, D_MODEL), lambda i, be, nu: (be[i], 0, 0))],
            out_specs=pl.BlockSpec((MOE_BLOCK, D_MODEL), lambda i, be, nu: (i, 0))),
        out_shape=jax.ShapeDtypeStruct((n_rows, D_MODEL), f32),
        compiler_params=_cparams(("arbitrary",)),
        name="moe_experts",
    )(block_e, n_used, xs, w_gate, w_up, w_down)


def _combine_body(d0_ref, d1_ref, x1_ref, gate_ref, nf_ref, ys_hbm, o_ref, y0_ref, y1_ref, sem):
    tc = d0_ref.shape[0]

    def row_copy(src, dst_ref, t):
        return pltpu.make_async_copy(ys_hbm.at[pl.ds(src, 1)], dst_ref.at[pl.ds(t, 1)], sem)

    def issue(t, c):
        row_copy(d0_ref[t], y0_ref, t).start()
        row_copy(d1_ref[t], y1_ref, t).start()
        return c

    lax.fori_loop(0, tc, issue, 0, unroll=DMA_UNROLL)

    def drain(t, c):
        row_copy(0, y0_ref, 0).wait()
        row_copy(0, y1_ref, 0).wait()
        return c

    lax.fori_loop(0, tc, drain, 0, unroll=DMA_UNROLL)
    gates = gate_ref[...]
    x2 = x1_ref[...] + gates[:, 0:1] * y0_ref[...] + gates[:, 1:2] * y1_ref[...]
    o_ref[...] = x2 * lax.rsqrt(jnp.mean(x2 * x2, axis=-1, keepdims=True) + NORM_EPS) * nf_ref[...]


def _combine(x1, gates, norm_final, ys, d0, d1):
    n = x1.shape[0]
    tc = min(256, n)
    tok = pl.BlockSpec((tc,), lambda i: (i,), memory_space=pltpu.SMEM)
    return pl.pallas_call(
        _combine_body,
        grid=(n // tc,),
        in_specs=[tok, tok,
                  pl.BlockSpec((tc, D_MODEL), lambda i: (i, 0)),
                  pl.BlockSpec((tc, LANES), lambda i: (i, 0)),
                  pl.BlockSpec((1, D_MODEL), lambda i: (0, 0)),
                  pl.BlockSpec(memory_space=pl.ANY)],
        out_specs=pl.BlockSpec((tc, D_MODEL), lambda i: (i, 0)),
        scratch_shapes=[pltpu.VMEM((tc, D_MODEL), f32), pltpu.VMEM((tc, D_MODEL), f32),
                        pltpu.SemaphoreType.DMA(())],
        out_shape=jax.ShapeDtypeStruct((n, D_MODEL), f32),
        compiler_params=_cparams(("arbitrary",)),
        name="moe_combine",
    )(d0, d1, x1, gates, norm_final, ys)


def _prep_weights(w_in, rw_mu, w_router_group, b_router_group, w_router_expert, b_router_expert):
    rw0 = ML_COLS
    g0 = ML_COLS + RW_COLS
    zeros = lambda c: jnp.zeros((D_MODEL, c), w_in.dtype)
    w_all = jnp.concatenate([
        w_in[:, 0:3072],
        w_in[:, rw0:rw0 + 3072],
        w_in[:, g0:g0 + 2048],
        w_in[:, 3072:3088], zeros(112),
        w_in[:, rw0 + 3072:rw0 + 3392], zeros(64)], axis=1).astype(bf16)
    mu_r = rw_mu[None, 0:1024]
    mu_k = rw_mu[None, 1024:2048]
    mu_v = rw_mu[None, 2048:3072]
    mu_l = jnp.concatenate([jnp.zeros((128,), f32), rw_mu[3072:3392], jnp.zeros((64,), f32)])[None]
    w_router = jnp.concatenate([w_router_group, w_router_expert,
                                jnp.zeros((D_MODEL, LANES - N_GROUPS - N_EXPERTS), f32)], axis=1)
    b_router = jnp.concatenate([b_router_group, b_router_expert,
                                jnp.zeros((LANES - N_GROUPS - N_EXPERTS,), f32)])[None]
    return w_all, mu_r, mu_k, mu_v, mu_l, w_router, b_router


def _pairs(p):
    return p.reshape(RW_PAIRS, 1, LANES)


def _forward(x, norm_mix, w_all, mus, ml_b_i, ml_b_f, ml_norm_g, rw_w0, rw_w2, rw_a0, rw_a2, rw_g2,
             rw_k_k, rw_k_a, rw_r_k, rw_gn_g, rw_gn_b, w_up_a, w_up_b, w_out, norm_ffn,
             w_router, b_router, w_gate, w_up, w_down, norm_final):
    B, T, _ = x.shape
    n = B * T
    x2d = x.reshape(n, D_MODEL)
    proj = _proj(x2d, norm_mix[None], w_all)
    proj3 = proj.reshape(B, T, W_ALL)
    gates_t = jnp.swapaxes(proj3[:, :, 8 * D_MODEL:8 * D_MODEL + 16], 1, 2)
    bias16 = jnp.concatenate([ml_b_i.reshape(-1), ml_b_f.reshape(-1)])
    bias_row = jnp.concatenate([bias16, jnp.zeros((LANES - 16,), f32)])[None]
    h_f, h_b = _mlstm(proj3, gates_t, bias_row, bias16[:, None])
    mu_r, mu_k, mu_v, mu_l = mus
    w2_hi = rw_w2.astype(bf16)
    w2_lo = (rw_w2 - w2_hi.astype(f32)).astype(bf16)
    r, k, v, a, lwf, lwb, g = _rw_prep(proj3, mu_r, mu_k, mu_v, mu_l, rw_w0, w2_hi, w2_lo, rw_a0[None],
                                       rw_a2, rw_g2)
    out_f, out_b = _rw_scan(r, k, v, a, lwf, lwb, _pairs(rw_k_k), _pairs(rw_k_a))
    rw_out = _rw_post(out_f, out_b, r, k, v, a, g, _pairs(rw_k_a), _pairs(rw_r_k), _pairs(rw_gn_g),
                      _pairs(rw_gn_b))
    x1, xn, logits = _merge(h_f.reshape(n, ML_V_W), h_b.reshape(n, ML_V_W), ml_norm_g[None],
                            rw_out.reshape(n, RW_W), proj, x2d,
                            w_up_a, w_up_b, w_out, norm_ffn[None], w_router, b_router)
    meta, gates, counts = _route(logits)
    counts = counts[0, :N_EXPERTS]
    padded = ((counts + MOE_BLOCK - 1) // MOE_BLOCK) * MOE_BLOCK
    pend = jnp.cumsum(padded)
    pstart = (pend - padded).astype(i32)
    nb = (2 * n) // MOE_BLOCK + N_EXPERTS
    block_start = jnp.arange(nb, dtype=i32) * MOE_BLOCK
    block_e = jnp.minimum(jnp.sum(block_start[:, None] >= pend[None, :], axis=1), N_EXPERTS - 1).astype(i32)
    n_used = (pend[-1:] // MOE_BLOCK).astype(i32)
    pstart_row = jnp.concatenate([pstart, jnp.zeros((LANES - N_EXPERTS,), i32)])[None]
    dest = _dest(meta, pstart_row)
    d0, d1 = dest[:, 0], dest[:, 1]
    xs = _scatter(xn, d0, d1, nb * MOE_BLOCK)
    ys = _experts(xs, block_e, n_used, w_gate, w_up, w_down)
    y = _combine(x1, gates, norm_final[None], ys, d0, d1)
    return y.reshape(B, T, D_MODEL)


def kernel(x_prompt, x_sample, norm_mix, w_in, ml_b_i, ml_b_f, ml_norm_g, rw_mu, rw_w0, rw_w2, rw_a0, rw_a2,
           rw_g2, rw_k_k, rw_k_a, rw_r_k, rw_gn_g, rw_gn_b, w_up_a, w_up_b, w_out, norm_ffn,
           w_router_group, b_router_group, w_router_expert, b_router_expert,
           w_expert_gate, w_expert_up, w_expert_down, norm_final):
    w_all, mu_r, mu_k, mu_v, mu_l, w_router, b_router = _prep_weights(
        w_in[0], rw_mu[0], w_router_group[0], b_router_group[0], w_router_expert[0], b_router_expert[0])
    args = (norm_mix[0], w_all, (mu_r, mu_k, mu_v, mu_l), ml_b_i[0], ml_b_f[0], ml_norm_g[0],
            rw_w0[0], rw_w2[0], rw_a0[0], rw_a2[0], rw_g2[0], rw_k_k[0], rw_k_a[0], rw_r_k[0].reshape(-1),
            rw_gn_g[0], rw_gn_b[0], w_up_a[0].astype(bf16), w_up_b[0].astype(bf16), w_out[0].astype(bf16),
            norm_ffn[0], w_router, b_router, w_expert_gate[0].astype(bf16), w_expert_up[0].astype(bf16),
            w_expert_down[0].astype(bf16), norm_final)
    return (_forward(x_prompt, *args), _forward(x_sample, *args))
```

```python
import functools

import jax
import jax.numpy as jnp
from jax import lax
from jax.experimental import pallas as pl
from jax.experimental.pallas import tpu as pltpu

f32 = jnp.float32
bf16 = jnp.bfloat16
i32 = jnp.int32

D_MODEL = 1024
ML_HEADS = 4
ML_DQK = 128
ML_DV = 256
ML_QK_W = ML_HEADS * ML_DQK
ML_V_W = ML_HEADS * ML_DV
ML_COLS = 2 * ML_QK_W + 2 * ML_V_W + 4 * ML_HEADS
RW_HEAD = 64
RW_HEADS = 16
RW_W = 1024
RW_PAIRS = RW_HEADS // 2
RW_COLS = 3 * RW_W + 64 + 64 + 64 + 128
N_GROUPS = 8
EXPERTS_PER_GROUP = 8
N_EXPERTS = 64
D_EXPERT = 512
NORM_EPS = 1e-6
RW_GN_EPS = 64e-5
L2_EPS = 1e-12

LANES = 128
SUBLANES = 8
W_ALL = 8 * D_MODEL + 512
SMALL_BLK = (8 * D_MODEL) // 512
ML_CHUNK = 128
RW_CHUNK = 64
MOE_BLOCK = 256
DMA_UNROLL = 8
VMEM_LIMIT = 56 * 1024 * 1024


def _cparams(sem):
    return pltpu.CompilerParams(dimension_semantics=sem, vmem_limit_bytes=VMEM_LIMIT)


def _dot(a, b):
    return jnp.dot(a.astype(bf16), b.astype(bf16), preferred_element_type=f32)


def _dot_nt(a, b):
    return lax.dot_general(a.astype(bf16), b.astype(bf16), (((1,), (1,)), ((), ())),
                           preferred_element_type=f32)


def _dot_tn(a, b):
    return lax.dot_general(a.astype(bf16), b.astype(bf16), (((0,), (0,)), ((), ())),
                           preferred_element_type=f32)


def _sigmoid(x):
    return 1.0 / (1.0 + jnp.exp(-x))


def _softplus(x):
    return jnp.maximum(x, 0.0) + jnp.log1p(jnp.exp(-jnp.abs(x)))


def _proj_body(x_ref, g_ref, w_ref, o_ref, xn_ref):
    @pl.when(pl.program_id(1) == 0)
    def _():
        x = x_ref[...]
        ms = jnp.mean(x * x, axis=-1, keepdims=True)
        xn_ref[...] = (x * lax.rsqrt(ms + NORM_EPS) * g_ref[...]).astype(bf16)

    o_ref[...] = jnp.dot(xn_ref[...], w_ref[...], preferred_element_type=f32)


def _proj(x2d, g, w_all):
    n = x2d.shape[0]
    tm = min(1024, n)
    tn = W_ALL // 4
    return pl.pallas_call(
        _proj_body,
        grid=(n // tm, W_ALL // tn),
        in_specs=[pl.BlockSpec((tm, D_MODEL), lambda i, j: (i, 0)),
                  pl.BlockSpec((1, D_MODEL), lambda i, j: (0, 0)),
                  pl.BlockSpec((D_MODEL, tn), lambda i, j: (0, j))],
        out_specs=pl.BlockSpec((tm, tn), lambda i, j: (i, j)),
        out_shape=jax.ShapeDtypeStruct((n, W_ALL), f32),
        scratch_shapes=[pltpu.VMEM((tm, D_MODEL), bf16)],
        compiler_params=_cparams(("parallel", "arbitrary")),
        name="proj",
    )(x2d, g, w_all)


def _log_sigmoid(x):
    return jnp.minimum(x, 0.0) - jnp.log1p(jnp.exp(-jnp.abs(x)))


def _mlstm_body(br_ref, bc_ref, qf_ref, kf_ref, vf_ref, gf_ref, gtf_ref,
                qb_ref, kb_ref, vb_ref, gb_ref, gtb_ref, hf_ref, hb_ref, c_ref, n_ref, m_ref):
    L = ML_CHUNK

    @pl.when(pl.program_id(1) == 0)
    def _():
        c_ref[...] = jnp.zeros_like(c_ref)
        n_ref[...] = jnp.zeros_like(n_ref)
        m_ref[...] = jnp.zeros_like(m_ref)

    rowi = lax.broadcasted_iota(i32, (L, L), 0)
    coli = lax.broadcasted_iota(i32, (L, L), 1)
    hp = lax.Precision.HIGHEST
    scale = ML_DQK ** -0.5
    units = []
    for d, (q_ref, k_ref, v_ref, g_ref, gt_ref, out_ref) in enumerate(
            ((qf_ref, kf_ref, vf_ref, gf_ref, gtf_ref, hf_ref), (qb_ref, kb_ref, vb_ref, gb_ref, gtb_ref, hb_ref))):
        mask = (coli >= rowi) if d else (coli <= rowi)
        tri = jnp.where(mask, 1.0, 0.0)
        g_all = g_ref[0] + br_ref[...]
        gt_all = gt_ref[0] + bc_ref[...]
        bcol_all = jnp.dot(tri, _log_sigmoid(g_all), precision=hp, preferred_element_type=f32)
        brow_all = lax.dot_general(_log_sigmoid(gt_all), tri, (((1,), (1,)), ((), ())), precision=hp,
                                   preferred_element_type=f32)
        for h in range(ML_HEADS):
            ii = 4 * d + h
            fi = 8 + 4 * d + h
            units.append(dict(
                d=d, h=h, mask=mask, out_ref=out_ref,
                i_col=g_all[:, ii:ii + 1], i_row=gt_all[ii:ii + 1, :],
                b_col=bcol_all[:, fi:fi + 1], b_row=brow_all[fi:fi + 1, :],
                q=q_ref[0, :, h * ML_DQK:(h + 1) * ML_DQK] * scale,
                k=k_ref[0, :, h * ML_DQK:(h + 1) * ML_DQK],
                v=v_ref[0, :, h * ML_DV:(h + 1) * ML_DV],
                c=c_ref[d, h], n=n_ref[d, h], m=m_ref[d, h]))
    for u in units:
        logw = jnp.where(u['mask'], u['b_col'] - u['b_row'] + u['i_row'], -jnp.inf)
        m_inter = u['b_col'] + u['m']
        u['m_t'] = jnp.maximum(m_inter, jnp.max(logw, axis=1, keepdims=True))
        u['dm'] = jnp.exp(logw - u['m_t'])
        u['inter'] = jnp.exp(m_inter - u['m_t'])
    qk = [_dot_nt(u['q'], u['k']) for u in units]
    s = [qk[i] * u['dm'] for i, u in enumerate(units)]
    sv = [_dot(s[i], u['v']) for i, u in enumerate(units)]
    qc = [_dot(u['q'], u['c']) for u in units]
    for i, u in enumerate(units):
        den = (jnp.sum(s[i], axis=1, keepdims=True)
               + u['inter'] * jnp.sum(u['q'] * u['n'], axis=1, keepdims=True))
        hh = (sv[i] + u['inter'] * qc[i]) / jnp.maximum(jnp.abs(den), jnp.exp(-u['m_t']))
        u['out_ref'][0, :, u['h'] * ML_DV:(u['h'] + 1) * ML_DV] = hh
        b_last = u['b_col'][0:1] if u['d'] else u['b_col'][L - 1:L]
        logu_col = b_last - u['b_col'] + u['i_col']
        logu_row = b_last - u['b_row'] + u['i_row']
        u['m_new'] = jnp.maximum(b_last + u['m'], jnp.max(logu_row, axis=1, keepdims=True))
        u['decay'] = jnp.exp(b_last + u['m'] - u['m_new'])
        u['ku'] = u['k'] * jnp.exp(logu_col - u['m_new'])
    kv = [_dot_tn(u['ku'], u['v']) for u in units]
    for i, u in enumerate(units):
        d, h = u['d'], u['h']
        c_ref[d, h] = u['decay'] * u['c'] + kv[i]
        n_ref[d, h] = u['decay'] * u['n'] + jnp.sum(u['ku'], axis=0, keepdims=True)
        m_ref[d, h] = u['m_new']


def _mlstm(proj3, gates_t, bias_row, bias_col):
    B, T, _ = proj3.shape
    L = ML_CHUNK
    nc = T // L

    def specs(cm):
        return [pl.BlockSpec((1, L, ML_QK_W), lambda b, c: (b, cm(c), 0)),
                pl.BlockSpec((1, L, ML_QK_W), lambda b, c: (b, cm(c), 1)),
                pl.BlockSpec((1, L, ML_V_W), lambda b, c: (b, cm(c), 1)),
                pl.BlockSpec((1, L, LANES), lambda b, c: (b, cm(c), (8 * D_MODEL) // LANES)),
                pl.BlockSpec((1, 16, L), lambda b, c: (b, 0, cm(c)))]

    fwd = lambda c: c
    bwd = lambda c: nc - 1 - c
    out = jax.ShapeDtypeStruct((B, T, ML_V_W), f32)
    return pl.pallas_call(
        _mlstm_body,
        grid=(B, nc),
        in_specs=[pl.BlockSpec((1, LANES), lambda b, c: (0, 0)), pl.BlockSpec((16, 1), lambda b, c: (0, 0))]
        + specs(fwd) + specs(bwd),
        out_specs=[pl.BlockSpec((1, L, ML_V_W), lambda b, c: (b, c, 0)),
                   pl.BlockSpec((1, L, ML_V_W), lambda b, c: (b, bwd(c), 0))],
        out_shape=[out, out],
        scratch_shapes=[pltpu.VMEM((2, ML_HEADS, ML_DQK, ML_DV), f32),
                        pltpu.VMEM((2, ML_HEADS, 1, ML_DQK), f32),
                        pltpu.VMEM((2, ML_HEADS, 1, 1), f32)],
        compiler_params=_cparams(("parallel", "arbitrary")),
        name="mlstm",
    )(bias_row, bias_col, *([proj3] * 4), gates_t, *([proj3] * 4), gates_t)


def _shift(p, prev8, next8, mu, first, last):
    tt = p.shape[0]
    rows = lax.broadcasted_iota(i32, p.shape, 0)
    prev_row = jnp.where(first, 0.0, prev8[SUBLANES - 1:SUBLANES])
    next_row = jnp.where(last, 0.0, next8[0:1])
    pm = jnp.where(rows == 0, prev_row, pltpu.roll(p, 1, axis=0))
    nx = jnp.where(rows == tt - 1, next_row, pltpu.roll(p, tt - 1, axis=0))
    return p + mu * (0.5 * (pm + nx) - p)


def _rw_prep_body(r_ref, rp_ref, rn_ref, k_ref, kp_ref, kn_ref, v_ref, vp_ref, vn_ref,
                  l_ref, lp_ref, ln_ref, mur_ref, muk_ref, muv_ref, mul_ref,
                  w0_ref, w2h_ref, w2l_ref, a0_ref, a2_ref, g2_ref,
                  ro_ref, ko_ref, vo_ref, ao_ref, lwf_ref, lwb_ref, go_ref):
    i = pl.program_id(1)
    first = i == 0
    last = i == pl.num_programs(1) - 1
    lo = _shift(l_ref[0], lp_ref[0], ln_ref[0], mul_ref[...], first, last)
    ad = lo[:, 256:320]
    sg = _sigmoid(lo[:, 320:448])
    th = [jnp.tanh(lo[:, 128:192]), jnp.tanh(lo[:, 192:256])]
    th_hi = [t.astype(bf16) for t in th]
    th_lo = [(t - h.astype(f32)).astype(bf16) for t, h in zip(th, th_hi)]

    def mm(a, b):
        return jnp.dot(a, b, preferred_element_type=f32)

    CW = 2 * LANES
    for cb in range(RW_W // CW):
        sl = slice(cb * CW, (cb + 1) * CW)
        cols = []
        for src, prv, nxt, mu in ((r_ref, rp_ref, rn_ref, mur_ref), (k_ref, kp_ref, kn_ref, muk_ref),
                                  (v_ref, vp_ref, vn_ref, muv_ref)):
            cols.append(_shift(src[0, :, sl], prv[0, :, sl], nxt[0, :, sl], mu[:, sl], first, last))
        lws = []
        for d in range(2):
            wl = w0_ref[d:d + 1, sl] + (mm(th_hi[d], w2h_ref[d, :, sl]) + mm(th_hi[d], w2l_ref[d, :, sl])
                                        + mm(th_lo[d], w2h_ref[d, :, sl]))
            lws.append(-jnp.exp(-_softplus(-wl) - 0.5))
        a = _sigmoid(a0_ref[:, sl] + _dot(ad, a2_ref[:, sl]))
        go_ref[0, :, sl] = _dot(sg, g2_ref[:, sl])
        for j in range(2):
            hp = 2 * cb + j
            half = slice(j * LANES, (j + 1) * LANES)
            ro_ref[0, hp] = cols[0][:, half]
            ko_ref[0, hp] = cols[1][:, half]
            vo_ref[0, hp] = cols[2][:, half]
            ao_ref[0, hp] = a[:, half]
            lwf_ref[0, hp] = lws[0][:, half]
            lwb_ref[0, hp] = lws[1][:, half]


def _rw_prep(proj3, mu_r, mu_k, mu_v, mu_l, w0, w2h, w2l, a0, a2, g2):
    B, T, _ = proj3.shape
    tt = min(256, T)
    nt = T // tt
    g8 = tt // SUBLANES
    n8 = T // SUBLANES

    def tile(width, blk):
        return [pl.BlockSpec((1, tt, width), lambda b, i: (b, i, blk)),
                pl.BlockSpec((1, SUBLANES, width), lambda b, i: (b, jnp.maximum(i * g8 - 1, 0), blk)),
                pl.BlockSpec((1, SUBLANES, width), lambda b, i: (b, jnp.minimum((i + 1) * g8, n8 - 1), blk))]

    full = lambda shape: pl.BlockSpec(shape, lambda b, i: (0,) * len(shape))
    in_specs = (tile(D_MODEL, 3) + tile(D_MODEL, 4) + tile(D_MODEL, 5) + tile(512, SMALL_BLK)
                + [full((1, RW_W))] * 3 + [full((1, 512)), full((2, RW_W)), full((2, 64, RW_W)), full((2, 64, RW_W)),
                                           full((1, RW_W)), full((64, RW_W)), full((128, RW_W))])
    hm = jax.ShapeDtypeStruct((B, RW_PAIRS, T, LANES), f32)
    hm_spec = pl.BlockSpec((1, RW_PAIRS, tt, LANES), lambda b, i: (b, 0, i, 0))
    return pl.pallas_call(
        _rw_prep_body,
        grid=(B, nt),
        in_specs=in_specs,
        out_specs=[hm_spec] * 6 + [pl.BlockSpec((1, tt, RW_W), lambda b, i: (b, i, 0))],
        out_shape=[hm] * 6 + [jax.ShapeDtypeStruct((B, T, RW_W), f32)],
        compiler_params=_cparams(("parallel", "parallel")),
        name="rw_prep",
    )(*([proj3] * 12), mu_r, mu_k, mu_v, mu_l, w0, w2h, w2l, a0, a2, g2)


def _chunk_cumsum(x, reverse):
    L = x.shape[0]
    rows = lax.broadcasted_iota(i32, x.shape, 0)
    s = 1
    while s < L:
        if reverse:
            x = x + jnp.where(rows < L - s, pltpu.roll(x, L - s, axis=0), 0.0)
        else:
            x = x + jnp.where(rows >= s, pltpu.roll(x, s, axis=0), 0.0)
        s *= 2
    return x


def _block_diag(x, lo):
    return jnp.concatenate([jnp.where(lo, x, 0.0), jnp.where(lo, 0.0, x)], axis=0)


def _rw_prepare(units, lo):
    L = units[0][0].shape[0]
    prepared = []
    for r, kp, v, kk, bh, lw, cum, reverse in units:
        tot = cum[0:1] if reverse else cum[L - 1:L]
        gi = jnp.exp(-cum)
        gl = jnp.exp(tot - cum)
        lhs = jnp.concatenate([-kk * jnp.exp(cum - lw), r * jnp.exp(cum)], axis=0)
        rhs = jnp.concatenate([_block_diag(bh * gi, lo), _block_diag(kp * gi, lo)], axis=0)
        tail = jnp.concatenate([bh * gl, kp * gl], axis=0)
        prepared.append((lhs.astype(bf16), rhs.astype(bf16), tail.astype(bf16),
                         _block_diag(v, lo).astype(bf16), v.astype(bf16), jnp.exp(tot)))
    return prepared


def _rw_solve(prepared, states, reverses, rowi, coli, lo):
    n = len(prepared)
    L = prepared[0][4].shape[0]
    G = [_dot_nt(prepared[i][0], prepared[i][1]) for i in range(n)]
    tok = yield None
    sv = [_dot_nt(prepared[i][0], states[i]) for i in range(n)]
    sv[0] = _anchor(sv[0], tok)
    tok = yield None
    p, a_rb, low = [], [], []
    for i in range(n):
        strict = (coli > rowi) if reverses[i] else (coli < rowi)
        incl = (coli >= rowi) if reverses[i] else (coli <= rowi)
        p.append(jnp.where(strict, G[i][:L, :2 * L], 0.0))
        a_rb.append(jnp.where(incl, G[i][L:, :2 * L], 0.0))
        low.append(jnp.concatenate([jnp.where(strict, G[i][:L, 2 * L:], 0.0),
                                    jnp.where(incl, G[i][L:, 2 * L:], 0.0)], axis=0))
    p[0] = _anchor(p[0], tok)
    av = [_dot(low[i], prepared[i][3]) for i in range(n)]
    tok = yield None
    y = [sv[i][:L] + av[i][:L] for i in range(n)]
    y[0] = _anchor(y[0], tok)
    s = 1
    while True:
        y = [y[i] + _dot(p[i], _block_diag(y[i], lo)) for i in range(n)]
        y[0] = _anchor(y[0], (yield None))
        s *= 2
        if s >= L:
            break
        p = [_dot(p[i], _block_diag(p[i], lo)) for i in range(n)]
        p[0] = _anchor(p[0], (yield None))
    ru = [_dot(a_rb[i], _block_diag(y[i], lo)) for i in range(n)]
    y[0] = _anchor(y[0], (yield None))
    upd = [_dot_tn(jnp.concatenate([y[i].astype(bf16), prepared[i][4]], axis=0), prepared[i][2])
           for i in range(n)]
    outs = [sv[i][L:] + av[i][L:] + ru[i] for i in range(n)]
    new_states = [states[i] * prepared[i][5] + upd[i] for i in range(n)]
    yield outs, new_states


def _zero_of(x):
    u = pltpu.bitcast(x[0:SUBLANES].astype(f32), jnp.uint32)
    z = lax.shift_right_logical(lax.shift_right_logical(u, jnp.uint32(16)), jnp.uint32(16))
    return pltpu.bitcast(z, f32)[0:1]


def _anchor(x, tok):
    return x if tok is None else x + tok


def _rw_scan_body(rf_ref, kf_ref, vf_ref, af_ref, lwf_ref, rb_ref, kb_ref, vb_ref, ab_ref, lwb_ref,
                  kk_ref, ka_ref, of_ref, ob_ref, s_ref, *slots):
    L = RW_CHUNK
    H = RW_HEAD
    i = pl.program_id(1)
    names = 6
    slot_a, slot_b = slots[:names], slots[names:]

    @pl.when(i == 0)
    def _():
        s_ref[...] = jnp.zeros_like(s_ref)
        for ref in slots:
            ref[...] = jnp.zeros_like(ref)

    rowi = lax.broadcasted_iota(i32, (L, 2 * L), 0)
    coli = lax.broadcasted_iota(i32, (L, 2 * L), 1) % L
    lo = lax.broadcasted_iota(i32, (1, LANES), 1) < H
    diag = (lax.broadcasted_iota(i32, (LANES, LANES), 0) < H) == (lax.broadcasted_iota(i32, (LANES, LANES), 1) < H)
    dirs = ((rf_ref, kf_ref, vf_ref, af_ref, lwf_ref), (rb_ref, kb_ref, vb_ref, ab_ref, lwb_ref))

    def prepare_pair(hp, dst):
        units = []
        k_k = kk_ref[hp]
        k_a = ka_ref[hp]
        for d, (r_ref, k_ref, v_ref, a_ref, lw_ref) in enumerate(dirs):
            k = k_ref[0, hp]
            a = a_ref[0, hp]
            lw = lw_ref[0, hp]
            cum = _chunk_cumsum(lw, d == 1)
            kk = k * k_k
            sq = kk * kk
            n0 = jnp.sum(jnp.where(lo, sq, 0.0), axis=-1, keepdims=True)
            n1 = jnp.sum(jnp.where(lo, 0.0, sq), axis=-1, keepdims=True)
            kk = kk / jnp.maximum(jnp.sqrt(jnp.where(lo, n0, n1)), L2_EPS)
            kp = k * (1.0 + (a - 1.0) * k_a)
            units.append((r_ref[0, hp], kp, v_ref[0, hp], kk, kk * a, lw, cum, d == 1))
        tok = None
        for d, ops in enumerate(_rw_prepare(units, lo)):
            for ref, val in zip(dst, ops):
                ref[2 * hp + d] = val
            tok = _zero_of(ops[2]) if tok is None else tok + _zero_of(ops[2])
        return tok

    def step(src, dst):
        prepared, states, reverses = [], [], []
        for hp in range(RW_PAIRS):
            for d in range(2):
                prepared.append(tuple(ref[2 * hp + d] for ref in src))
                states.append(s_ref[d, hp])
                reverses.append(d == 1)
        todo = list(range(RW_PAIRS))
        solve = _rw_solve(prepared, states, reverses, rowi, coli, lo)
        result = next(solve)
        while result is None:
            result = solve.send(prepare_pair(todo.pop(0), dst) if todo else None)
        for hp in todo:
            prepare_pair(hp, dst)
        outs, new_states = result
        for hp in range(RW_PAIRS):
            for d, o_ref in enumerate((of_ref, ob_ref)):
                o_ref[0, hp] = outs[2 * hp + d]
                s_ref[d, hp] = jnp.where(diag, new_states[2 * hp + d], 0.0)

    @pl.when(i % 2 == 0)
    def _():
        step(slot_a, slot_b)

    @pl.when(i % 2 == 1)
    def _():
        step(slot_b, slot_a)


def _rw_scan(r, k, v, a, lwf, lwb, k_k, k_a):
    B, _, T, _ = r.shape
    L = RW_CHUNK
    nc = T // L
    last = nc - 1
    fin = pl.BlockSpec((1, RW_PAIRS, L, LANES), lambda b, i: (b, 0, jnp.minimum(i, last), 0))
    bin_ = pl.BlockSpec((1, RW_PAIRS, L, LANES), lambda b, i: (b, 0, last - jnp.minimum(i, last), 0))
    fout = pl.BlockSpec((1, RW_PAIRS, L, LANES), lambda b, i: (b, 0, jnp.maximum(i - 1, 0), 0))
    bout = pl.BlockSpec((1, RW_PAIRS, L, LANES), lambda b, i: (b, 0, last - jnp.maximum(i - 1, 0), 0))
    ps = pl.BlockSpec((RW_PAIRS, 1, LANES), lambda b, i: (0, 0, 0))
    hm = jax.ShapeDtypeStruct((B, RW_PAIRS, T, LANES), f32)
    nu = 2 * RW_PAIRS
    slot = [pltpu.VMEM((nu, 2 * L, LANES), bf16), pltpu.VMEM((nu, 4 * L, LANES), bf16),
            pltpu.VMEM((nu, 2 * L, LANES), bf16), pltpu.VMEM((nu, 2 * L, LANES), bf16),
            pltpu.VMEM((nu, L, LANES), bf16), pltpu.VMEM((nu, 1, LANES), f32)]
    return pl.pallas_call(
        _rw_scan_body,
        grid=(B, nc + 1),
        in_specs=[fin] * 5 + [bin_] * 5 + [ps, ps],
        out_specs=[fout, bout],
        out_shape=[hm, hm],
        scratch_shapes=[pltpu.VMEM((2, RW_PAIRS, LANES, LANES), f32)] + slot + slot,
        compiler_params=_cparams(("parallel", "arbitrary")),
        name="rw_scan",
    )(r, k, v, a, lwf, r, k, v, a, lwb, k_k, k_a)


def _head_mean(x, avg):
    hi = x.astype(bf16)
    lo = (x - hi.astype(f32)).astype(bf16)
    return (jnp.dot(hi, avg, preferred_element_type=f32) + jnp.dot(lo, avg, preferred_element_type=f32))


def _rw_post_body(of_ref, ob_ref, r_ref, k_ref, v_ref, a_ref, g_ref, ka_ref, rk_ref, gg_ref, gb_ref, o_ref):
    same_head = (lax.broadcasted_iota(i32, (LANES, LANES), 0) < RW_HEAD) == (
        lax.broadcasted_iota(i32, (LANES, LANES), 1) < RW_HEAD)
    avg = jnp.where(same_head, 1.0 / RW_HEAD, 0.0).astype(bf16)
    for hp in range(RW_PAIRS):
        out = of_ref[0, hp] + ob_ref[0, hp]
        kp = k_ref[0, hp] * (1.0 + (a_ref[0, hp] - 1.0) * ka_ref[hp])
        rk = r_ref[0, hp] * kp * rk_ref[hp]
        cen = out - _head_mean(out, avg)
        var = _head_mean(cen * cen, avg)
        normed = cen * lax.rsqrt(var + RW_GN_EPS)
        bonus = (_head_mean(rk, avg) * RW_HEAD) * v_ref[0, hp]
        lanes = slice(hp * LANES, (hp + 1) * LANES)
        o_ref[0, :, lanes] = (normed * gg_ref[hp] + gb_ref[hp] + bonus) * g_ref[0, :, lanes]


def _rw_post(out_f, out_b, r, k, v, a, g, k_a, r_k, gn_g, gn_b):
    B, _, T, _ = r.shape
    tt = min(256, T)
    hs = pl.BlockSpec((1, RW_PAIRS, tt, LANES), lambda b, i: (b, 0, i, 0))
    ps = pl.BlockSpec((RW_PAIRS, 1, LANES), lambda b, i: (0, 0, 0))
    ts = pl.BlockSpec((1, tt, RW_W), lambda b, i: (b, i, 0))
    return pl.pallas_call(
        _rw_post_body,
        grid=(B, T // tt),
        in_specs=[hs] * 6 + [ts] + [ps] * 4,
        out_specs=ts,
        out_shape=jax.ShapeDtypeStruct((B, T, RW_W), f32),
        compiler_params=_cparams(("parallel", "parallel")),
        name="rw_post",
    )(out_f, out_b, r, k, v, a, g, k_a, r_k, gn_g, gn_b)


def _merge_body(hf_ref, hb_ref, o_ref, ng_ref, rw_ref, ga_ref, gb_ref, x_ref, wa_ref, wb_ref, wo_ref, nf_ref,
                wr_ref, br_ref, x1_ref, xn_ref, lg_ref):
    heads = []
    for h in range(ML_HEADS):
        sl = slice(h * ML_DV, (h + 1) * ML_DV)
        t = hf_ref[:, sl] + hb_ref[:, sl]
        t = t * lax.rsqrt(jnp.mean(t * t, axis=-1, keepdims=True) + NORM_EPS)
        heads.append((_sigmoid(o_ref[:, sl]) * (t * ng_ref[:, sl])).astype(bf16))
    ml = jnp.concatenate(heads, axis=1)
    y_a = jnp.dot(ml, wa_ref[...], preferred_element_type=f32)
    y_b = jnp.dot(rw_ref[...].astype(bf16), wb_ref[...], preferred_element_type=f32)
    merged = _sigmoid(ga_ref[...]) * y_a + _sigmoid(gb_ref[...]) * y_b
    x1 = x_ref[...] + jnp.dot(merged.astype(bf16), wo_ref[...], preferred_element_type=f32)
    x1_ref[...] = x1
    xn = x1 * lax.rsqrt(jnp.mean(x1 * x1, axis=-1, keepdims=True) + NORM_EPS) * nf_ref[...]
    xn_ref[...] = xn
    lg_ref[...] = jnp.dot(xn, wr_ref[...], precision=lax.Precision.HIGHEST,
                          preferred_element_type=f32) + br_ref[...]


def _merge(h_f, h_b, ml_norm_g, rw_out, proj, x2d, w_up_a, w_up_b, w_out, norm_ffn, w_router, b_router):
    n = x2d.shape[0]
    tm = min(512, n)
    row = lambda blk: pl.BlockSpec((tm, D_MODEL), lambda i: (i, blk))
    full = lambda shape: pl.BlockSpec(shape, lambda i: (0,) * len(shape))
    return pl.pallas_call(
        _merge_body,
        grid=(n // tm,),
        in_specs=[row(0), row(0), row(2), full((1, ML_V_W)), row(0), row(6), row(7), row(0),
                  full((D_MODEL, D_MODEL)), full((D_MODEL, D_MODEL)), full((D_MODEL, D_MODEL)),
                  full((1, D_MODEL)), full((D_MODEL, LANES)), full((1, LANES))],
        out_specs=[row(0), row(0), pl.BlockSpec((tm, LANES), lambda i: (i, 0))],
        out_shape=[jax.ShapeDtypeStruct((n, D_MODEL), f32), jax.ShapeDtypeStruct((n, D_MODEL), f32),
                   jax.ShapeDtypeStruct((n, LANES), f32)],
        compiler_params=_cparams(("parallel",)),
        name="merge",
    )(h_f, h_b, proj, ml_norm_g, rw_out, proj, proj, x2d, w_up_a, w_up_b, w_out, norm_ffn, w_router, b_router)


def _first_argmax(p, width):
    lane = lax.broadcasted_iota(i32, p.shape, 1)
    top = jnp.max(p, axis=-1, keepdims=True)
    idx = jnp.min(jnp.where(p == top, lane, width), axis=-1, keepdims=True)
    return top, idx, lane


def _route_body(lg_ref, meta_ref, gate_ref, cnt_ref, carry_ref):
    @pl.when(pl.program_id(0) == 0)
    def _():
        carry_ref[...] = jnp.zeros_like(carry_ref)

    lg = lg_ref[...]
    tr = lg.shape[0]
    gl = lg[:, 0:N_GROUPS]
    ge = jnp.exp(gl - jnp.max(gl, axis=-1, keepdims=True))
    p_group = ge / jnp.sum(ge, axis=-1, keepdims=True)
    p_g, g_idx, _ = _first_argmax(p_group, N_GROUPS)
    el = jnp.zeros((tr, EXPERTS_PER_GROUP), f32)
    for g in range(N_GROUPS):
        lo = N_GROUPS + g * EXPERTS_PER_GROUP
        el = el + jnp.where(g_idx == g, 1.0, 0.0) * lg[:, lo:lo + EXPERTS_PER_GROUP]
    ee = jnp.exp(el - jnp.max(el, axis=-1, keepdims=True))
    p_exp = ee / jnp.sum(ee, axis=-1, keepdims=True)
    p1, i1, lane8 = _first_argmax(p_exp, EXPERTS_PER_GROUP)
    p2, i2, _ = _first_argmax(jnp.where(lane8 == i1, -1.0, p_exp), EXPERTS_PER_GROUP)
    psum = p1 + p2
    g1 = p_g * p1 / psum
    g2 = p_g * p2 / psum
    e0 = g_idx * EXPERTS_PER_GROUP + i1
    e1 = g_idx * EXPERTS_PER_GROUP + i2
    lane = lax.broadcasted_iota(i32, (tr, LANES), 1)
    hit0 = lane == e0
    hit1 = lane == e1
    onehot = jnp.where(hit0 | hit1, 1.0, 0.0)
    rowi = lax.broadcasted_iota(i32, (tr, tr), 0)
    coli = lax.broadcasted_iota(i32, (tr, tr), 1)
    before = jnp.where(coli < rowi, 1.0, 0.0)
    prefix = _dot(before, onehot) + carry_ref[...]
    r0 = jnp.sum(jnp.where(hit0, prefix, 0.0), axis=-1, keepdims=True).astype(i32)
    r1 = jnp.sum(jnp.where(hit1, prefix, 0.0), axis=-1, keepdims=True).astype(i32)
    carry = carry_ref[...] + jnp.sum(onehot, axis=0, keepdims=True)
    carry_ref[...] = carry
    cnt_ref[...] = carry.astype(i32)
    meta_ref[...] = jnp.where(lane == 0, e0, jnp.where(lane == 1, e1, jnp.where(lane == 2, r0, r1)))
    gate_ref[...] = jnp.where(lane == 0, g1, g2)


def _route(logits):
    n = logits.shape[0]
    tr = min(512, n)
    return pl.pallas_call(
        _route_body,
        grid=(n // tr,),
        in_specs=[pl.BlockSpec((tr, LANES), lambda i: (i, 0))],
        out_specs=[pl.BlockSpec((tr, LANES), lambda i: (i, 0)), pl.BlockSpec((tr, LANES), lambda i: (i, 0)),
                   pl.BlockSpec((1, LANES), lambda i: (0, 0))],
        out_shape=[jax.ShapeDtypeStruct((n, LANES), i32), jax.ShapeDtypeStruct((n, LANES), f32),
                   jax.ShapeDtypeStruct((1, LANES), i32)],
        scratch_shapes=[pltpu.VMEM((1, LANES), f32)],
        compiler_params=_cparams(("arbitrary",)),
        name="route",
    )(logits)


def _dest_body(meta_ref, ps_ref, o_ref):
    meta = meta_ref[...]
    lane = lax.broadcasted_iota(i32, meta.shape, 1)
    ps = ps_ref[...].astype(f32)
    seg0 = jnp.sum(jnp.where(lane == meta[:, 0:1], ps, 0.0), axis=-1, keepdims=True).astype(i32)
    seg1 = jnp.sum(jnp.where(lane == meta[:, 1:2], ps, 0.0), axis=-1, keepdims=True).astype(i32)
    o_ref[...] = jnp.where(lane == 0, seg0 + meta[:, 2:3], seg1 + meta[:, 3:4])


def _dest(meta, pstart_row):
    n = meta.shape[0]
    tr = min(1024, n)
    return pl.pallas_call(
        _dest_body,
        grid=(n // tr,),
        in_specs=[pl.BlockSpec((tr, LANES), lambda i: (i, 0)), pl.BlockSpec((1, LANES), lambda i: (0, 0))],
        out_specs=pl.BlockSpec((tr, LANES), lambda i: (i, 0)),
        out_shape=jax.ShapeDtypeStruct((n, LANES), i32),
        compiler_params=_cparams(("parallel",)),
        name="moe_dest",
    )(meta, pstart_row)


def _scatter_body(d0_ref, d1_ref, x_ref, buf_in, xs_hbm, sem):
    del buf_in
    ts = d0_ref.shape[0]

    def row_copy(t, dst):
        return pltpu.make_async_copy(x_ref.at[pl.ds(t, 1)], xs_hbm.at[pl.ds(dst, 1)], sem)

    def issue(t, c):
        row_copy(t, d0_ref[t]).start()
        row_copy(t, d1_ref[t]).start()
        return c

    lax.fori_loop(0, ts, issue, 0, unroll=DMA_UNROLL)

    def drain(t, c):
        row_copy(0, 0).wait()
        row_copy(0, 0).wait()
        return c

    lax.fori_loop(0, ts, drain, 0, unroll=DMA_UNROLL)


def _scatter(xn, d0, d1, n_rows):
    n = xn.shape[0]
    ts = min(512, n)
    tok = pl.BlockSpec((ts,), lambda i: (i,), memory_space=pltpu.SMEM)
    anyspec = pl.BlockSpec(memory_space=pl.ANY)
    buf = jnp.zeros((n_rows, D_MODEL), f32)
    return pl.pallas_call(
        _scatter_body,
        grid=(n // ts,),
        in_specs=[tok, tok, pl.BlockSpec((ts, D_MODEL), lambda i: (i, 0)), anyspec],
        out_specs=anyspec,
        scratch_shapes=[pltpu.SemaphoreType.DMA(())],
        out_shape=jax.ShapeDtypeStruct((n_rows, D_MODEL), f32),
        input_output_aliases={3: 0},
        compiler_params=_cparams(("arbitrary",)),
        name="moe_scatter",
    )(d0, d1, xn, buf)


def _experts_body(be_ref, nu_ref, xs_ref, wg_ref, wu_ref, wd_ref, ys_ref):
    del be_ref

    @pl.when(pl.program_id(0) < nu_ref[0])
    def _():
        xb = xs_ref[...].astype(bf16)
        hg = jnp.dot(xb, wg_ref[0], preferred_element_type=f32)
        hu = jnp.dot(xb, wu_ref[0], preferred_element_type=f32)
        hb = hg * _sigmoid(hg) * hu
        ys_ref[...] = jnp.dot(hb.astype(bf16), wd_ref[0], preferred_element_type=f32)

    @pl.when(pl.program_id(0) >= nu_ref[0])
    def _():
        ys_ref[...] = jnp.zeros_like(ys_ref)


def _experts(xs, block_e, n_used, w_gate, w_up, w_down):
    n_rows = xs.shape[0]
    nb = n_rows // MOE_BLOCK
    return pl.pallas_call(
        _experts_body,
        grid_spec=pltpu.PrefetchScalarGridSpec(
            num_scalar_prefetch=2, grid=(nb,),
            in_specs=[pl.BlockSpec((MOE_BLOCK, D_MODEL), lambda i, be, nu: (i, 0)),
                      pl.BlockSpec((1, D_MODEL, D_EXPERT), lambda i, be, nu: (be[i], 0, 0)),
                      pl.BlockSpec((1, D_MODEL, D_EXPERT), lambda i, be, nu: (be[i], 0, 0)),
                      pl.BlockSpec((1, D_EXPERT, D_MODEL), lambda i, be, nu: (be[i], 0, 0))],
            out_specs=pl.BlockSpec((MOE_BLOCK, D_MODEL), lambda i, be, nu: (i, 0))),
        out_shape=jax.ShapeDtypeStruct((n_rows, D_MODEL), f32),
        compiler_params=_cparams(("arbitrary",)),
        name="moe_experts",
    )(block_e, n_used, xs, w_gate, w_up, w_down)


def _combine_body(d0_ref, d1_ref, x1_ref, gate_ref, nf_ref, ys_hbm, o_ref, y0_ref, y1_ref, sem):
    tc = d0_ref.shape[0]

    def row_copy(src, dst_ref, t):
        return pltpu.make_async_copy(ys_hbm.at[pl.ds(src, 1)], dst_ref.at[pl.ds(t, 1)], sem)

    def issue(t, c):
        row_copy(d0_ref[t], y0_ref, t).start()
        row_copy(d1_ref[t], y1_ref, t).start()
        return c

    lax.fori_loop(0, tc, issue, 0, unroll=DMA_UNROLL)

    def drain(t, c):
        row_copy(0, y0_ref, 0).wait()
        row_copy(0, y1_ref, 0).wait()
        return c

    lax.fori_loop(0, tc, drain, 0, unroll=DMA_UNROLL)
    gates = gate_ref[...]
    x2 = x1_ref[...] + gates[:, 0:1] * y0_ref[...] + gates[:, 1:2] * y1_ref[...]
    o_ref[...] = x2 * lax.rsqrt(jnp.mean(x2 * x2, axis=-1, keepdims=True) + NORM_EPS) * nf_ref[...]


def _combine(x1, gates, norm_final, ys, d0, d1):
    n = x1.shape[0]
    tc = min(256, n)
    tok = pl.BlockSpec((tc,), lambda i: (i,), memory_space=pltpu.SMEM)
    return pl.pallas_call(
        _combine_body,
        grid=(n // tc,),
        in_specs=[tok, tok,
                  pl.BlockSpec((tc, D_MODEL), lambda i: (i, 0)),
                  pl.BlockSpec((tc, LANES), lambda i: (i, 0)),
                  pl.BlockSpec((1, D_MODEL), lambda i: (0, 0)),
                  pl.BlockSpec(memory_space=pl.ANY)],
        out_specs=pl.BlockSpec((tc, D_MODEL), lambda i: (i, 0)),
        scratch_shapes=[pltpu.VMEM((tc, D_MODEL), f32), pltpu.VMEM((tc, D_MODEL), f32),
                        pltpu.SemaphoreType.DMA(())],
        out_shape=jax.ShapeDtypeStruct((n, D_MODEL), f32),
        compiler_params=_cparams(("arbitrary",)),
        name="moe_combine",
    )(d0, d1, x1, gates, norm_final, ys)


def _prep_weights(w_in, rw_mu, w_router_group, b_router_group, w_router_expert, b_router_expert):
    rw0 = ML_COLS
    g0 = ML_COLS + RW_COLS
    zeros = lambda c: jnp.zeros((D_MODEL, c), w_in.dtype)
    w_all = jnp.concatenate([
        w_in[:, 0:3072],
        w_in[:, rw0:rw0 + 3072],
        w_in[:, g0:g0 + 2048],
        w_in[:, 3072:3088], zeros(112),
        w_in[:, rw0 + 3072:rw0 + 3392], zeros(64)], axis=1).astype(bf16)
    mu_r = rw_mu[None, 0:1024]
    mu_k = rw_mu[None, 1024:2048]
    mu_v = rw_mu[None, 2048:3072]
    mu_l = jnp.concatenate([jnp.zeros((128,), f32), rw_mu[3072:3392], jnp.zeros((64,), f32)])[None]
    w_router = jnp.concatenate([w_router_group, w_router_expert,
                                jnp.zeros((D_MODEL, LANES - N_GROUPS - N_EXPERTS), f32)], axis=1)
    b_router = jnp.concatenate([b_router_group, b_router_expert,
                                jnp.zeros((LANES - N_GROUPS - N_EXPERTS,), f32)])[None]
    return w_all, mu_r, mu_k, mu_v, mu_l, w_router, b_router


def _pairs(p):
    return p.reshape(RW_PAIRS, 1, LANES)


def _forward(x, norm_mix, w_all, mus, ml_b_i, ml_b_f, ml_norm_g, rw_w0, rw_w2, rw_a0, rw_a2, rw_g2,
             rw_k_k, rw_k_a, rw_r_k, rw_gn_g, rw_gn_b, w_up_a, w_up_b, w_out, norm_ffn,
             w_router, b_router, w_gate, w_up, w_down, norm_final):
    B, T, _ = x.shape
    n = B * T
    x2d = x.reshape(n, D_MODEL)
    proj = _proj(x2d, norm_mix[None], w_all)
    proj3 = proj.reshape(B, T, W_ALL)
    gates_t = jnp.swapaxes(proj3[:, :, 8 * D_MODEL:8 * D_MODEL + 16], 1, 2)
    bias16 = jnp.concatenate([ml_b_i.reshape(-1), ml_b_f.reshape(-1)])
    bias_row = jnp.concatenate([bias16, jnp.zeros((LANES - 16,), f32)])[None]
    h_f, h_b = _mlstm(proj3, gates_t, bias_row, bias16[:, None])
    mu_r, mu_k, mu_v, mu_l = mus
    w2_hi = rw_w2.astype(bf16)
    w2_lo = (rw_w2 - w2_hi.astype(f32)).astype(bf16)
    r, k, v, a, lwf, lwb, g = _rw_prep(proj3, mu_r, mu_k, mu_v, mu_l, rw_w0, w2_hi, w2_lo, rw_a0[None],
                                       rw_a2, rw_g2)
    out_f, out_b = _rw_scan(r, k, v, a, lwf, lwb, _pairs(rw_k_k), _pairs(rw_k_a))
    rw_out = _rw_post(out_f, out_b, r, k, v, a, g, _pairs(rw_k_a), _pairs(rw_r_k), _pairs(rw_gn_g),
                      _pairs(rw_gn_b))
    x1, xn, logits = _merge(h_f.reshape(n, ML_V_W), h_b.reshape(n, ML_V_W), ml_norm_g[None],
                            rw_out.reshape(n, RW_W), proj, x2d,
                            w_up_a, w_up_b, w_out, norm_ffn[None], w_router, b_router)
    meta, gates, counts = _route(logits)
    counts = counts[0, :N_EXPERTS]
    padded = ((counts + MOE_BLOCK - 1) // MOE_BLOCK) * MOE_BLOCK
    pend = jnp.cumsum(padded)
    pstart = (pend - padded).astype(i32)
    nb = (2 * n) // MOE_BLOCK + N_EXPERTS
    block_start = jnp.arange(nb, dtype=i32) * MOE_BLOCK
    block_e = jnp.minimum(jnp.sum(block_start[:, None] >= pend[None, :], axis=1), N_EXPERTS - 1).astype(i32)
    n_used = (pend[-1:] // MOE_BLOCK).astype(i32)
    pstart_row = jnp.concatenate([pstart, jnp.zeros((LANES - N_EXPERTS,), i32)])[None]
    dest = _dest(meta, pstart_row)
    d0, d1 = dest[:, 0], dest[:, 1]
    xs = _scatter(xn, d0, d1, nb * MOE_BLOCK)
    ys = _experts(xs, block_e, n_used, w_gate, w_up, w_down)
    y = _combine(x1, gates, norm_final[None], ys, d0, d1)
    return y.reshape(B, T, D_MODEL)


def kernel(x_prompt, x_sample, norm_mix, w_in, ml_b_i, ml_b_f, ml_norm_g, rw_mu, rw_w0, rw_w2, rw_a0, rw_a2,
           rw_g2, rw_k_k, rw_k_a, rw_r_k, rw_gn_g, rw_gn_b, w_up_a, w_up_b, w_out, norm_ffn,
           w_router_group, b_router_group, w_router_expert, b_router_expert,
           w_expert_gate, w_expert_up, w_expert_down, norm_final):
    w_all, mu_r, mu_k, mu_v, mu_l, w_router, b_router = _prep_weights(
        w_in[0], rw_mu[0], w_router_group[0], b_router_group[0], w_router_expert[0], b_router_expert[0])
    args = (norm_mix[0], w_all, (mu_r, mu_k, mu_v, mu_l), ml_b_i[0], ml_b_f[0], ml_norm_g[0],
            rw_w0[0], rw_w2[0], rw_a0[0], rw_a2[0], rw_g2[0], rw_k_k[0], rw_k_a[0], rw_r_k[0].reshape(-1),
            rw_gn_g[0], rw_gn_b[0], w_up_a[0].astype(bf16), w_up_b[0].astype(bf16), w_out[0].astype(bf16),
            norm_ffn[0], w_router, b_router, w_expert_gate[0].astype(bf16), w_expert_up[0].astype(bf16),
            w_expert_down[0].astype(bf16), norm_final)
    return (_forward(x_prompt, *args), _forward(x_sample, *args))
```

```python
import functools

import jax
import jax.numpy as jnp
from jax import lax
from jax.experimental import pallas as pl
from jax.experimental.pallas import tpu as pltpu

f32 = jnp.float32
bf16 = jnp.bfloat16
i32 = jnp.int32

D_MODEL = 1024
ML_HEADS = 4
ML_DQK = 128
ML_DV = 256
ML_QK_W = ML_HEADS * ML_DQK
ML_V_W = ML_HEADS * ML_DV
ML_COLS = 2 * ML_QK_W + 2 * ML_V_W + 4 * ML_HEADS
RW_HEAD = 64
RW_HEADS = 16
RW_W = 1024
RW_PAIRS = RW_HEADS // 2
RW_COLS = 3 * RW_W + 64 + 64 + 64 + 128
N_GROUPS = 8
EXPERTS_PER_GROUP = 8
N_EXPERTS = 64
D_EXPERT = 512
NORM_EPS = 1e-6
RW_GN_EPS = 64e-5
L2_EPS = 1e-12

LANES = 128
SUBLANES = 8
W_ALL = 8 * D_MODEL + 512
SMALL_BLK = (8 * D_MODEL) // 512
ML_CHUNK = 128
RW_CHUNK = 64
MOE_BLOCK = 256
DMA_UNROLL = 8
VMEM_LIMIT = 56 * 1024 * 1024


def _cparams(sem):
    return pltpu.CompilerParams(dimension_semantics=sem, vmem_limit_bytes=VMEM_LIMIT)


def _dot(a, b):
    return jnp.dot(a.astype(bf16), b.astype(bf16), preferred_element_type=f32)


def _dot_nt(a, b):
    return lax.dot_general(a.astype(bf16), b.astype(bf16), (((1,), (1,)), ((), ())),
                           preferred_element_type=f32)


def _dot_tn(a, b):
    return lax.dot_general(a.astype(bf16), b.astype(bf16), (((0,), (0,)), ((), ())),
                           preferred_element_type=f32)


def _sigmoid(x):
    return 1.0 / (1.0 + jnp.exp(-x))


def _softplus(x):
    return jnp.maximum(x, 0.0) + jnp.log1p(jnp.exp(-jnp.abs(x)))


def _proj_body(x_ref, g_ref, w_ref, o_ref, xn_ref):
    @pl.when(pl.program_id(1) == 0)
    def _():
        x = x_ref[...]
        ms = jnp.mean(x * x, axis=-1, keepdims=True)
        xn_ref[...] = (x * lax.rsqrt(ms + NORM_EPS) * g_ref[...]).astype(bf16)

    o_ref[...] = jnp.dot(xn_ref[...], w_ref[...], preferred_element_type=f32)


def _proj(x2d, g, w_all):
    n = x2d.shape[0]
    tm = min(1024, n)
    tn = W_ALL // 4
    return pl.pallas_call(
        _proj_body,
        grid=(n // tm, W_ALL // tn),
        in_specs=[pl.BlockSpec((tm, D_MODEL), lambda i, j: (i, 0)),
                  pl.BlockSpec((1, D_MODEL), lambda i, j: (0, 0)),
                  pl.BlockSpec((D_MODEL, tn), lambda i, j: (0, j))],
        out_specs=pl.BlockSpec((tm, tn), lambda i, j: (i, j)),
        out_shape=jax.ShapeDtypeStruct((n, W_ALL), f32),
        scratch_shapes=[pltpu.VMEM((tm, D_MODEL), bf16)],
        compiler_params=_cparams(("parallel", "arbitrary")),
        name="proj",
    )(x2d, g, w_all)


def _log_sigmoid(x):
    return jnp.minimum(x, 0.0) - jnp.log1p(jnp.exp(-jnp.abs(x)))


def _mlstm_body(br_ref, bc_ref, qf_ref, kf_ref, vf_ref, gf_ref, gtf_ref,
                qb_ref, kb_ref, vb_ref, gb_ref, gtb_ref, hf_ref, hb_ref, c_ref, n_ref, m_ref):
    L = ML_CHUNK

    @pl.when(pl.program_id(1) == 0)
    def _():
        c_ref[...] = jnp.zeros_like(c_ref)
        n_ref[...] = jnp.zeros_like(n_ref)
        m_ref[...] = jnp.zeros_like(m_ref)

    rowi = lax.broadcasted_iota(i32, (L, L), 0)
    coli = lax.broadcasted_iota(i32, (L, L), 1)
    hp = lax.Precision.HIGHEST
    scale = ML_DQK ** -0.5
    units = []
    for d, (q_ref, k_ref, v_ref, g_ref, gt_ref, out_ref) in enumerate(
            ((qf_ref, kf_ref, vf_ref, gf_ref, gtf_ref, hf_ref), (qb_ref, kb_ref, vb_ref, gb_ref, gtb_ref, hb_ref))):
        mask = (coli >= rowi) if d else (coli <= rowi)
        tri = jnp.where(mask, 1.0, 0.0)
        g_all = g_ref[0] + br_ref[...]
        gt_all = gt_ref[0] + bc_ref[...]
        bcol_all = jnp.dot(tri, _log_sigmoid(g_all), precision=hp, preferred_element_type=f32)
        brow_all = lax.dot_general(_log_sigmoid(gt_all), tri, (((1,), (1,)), ((), ())), precision=hp,
                                   preferred_element_type=f32)
        for h in range(ML_HEADS):
            ii = 4 * d + h
            fi = 8 + 4 * d + h
            units.append(dict(
                d=d, h=h, mask=mask, out_ref=out_ref,
                i_col=g_all[:, ii:ii + 1], i_row=gt_all[ii:ii + 1, :],
                b_col=bcol_all[:, fi:fi + 1], b_row=brow_all[fi:fi + 1, :],
                q=q_ref[0, :, h * ML_DQK:(h + 1) * ML_DQK] * scale,
                k=k_ref[0, :, h * ML_DQK:(h + 1) * ML_DQK],
                v=v_ref[0, :, h * ML_DV:(h + 1) * ML_DV],
                c=c_ref[d, h], n=n_ref[d, h], m=m_ref[d, h]))
    for u in units:
        logw = jnp.where(u['mask'], u['b_col'] - u['b_row'] + u['i_row'], -jnp.inf)
        m_inter = u['b_col'] + u['m']
        u['m_t'] = jnp.maximum(m_inter, jnp.max(logw, axis=1, keepdims=True))
        u['dm'] = jnp.exp(logw - u['m_t'])
        u['inter'] = jnp.exp(m_inter - u['m_t'])
    qk = [_dot_nt(u['q'], u['k']) for u in units]
    s = [qk[i] * u['dm'] for i, u in enumerate(units)]
    sv = [_dot(s[i], u['v']) for i, u in enumerate(units)]
    qc = [_dot(u['q'], u['c']) for u in units]
    for i, u in enumerate(units):
        den = (jnp.sum(s[i], axis=1, keepdims=True)
               + u['inter'] * jnp.sum(u['q'] * u['n'], axis=1, keepdims=True))
        hh = (sv[i] + u['inter'] * qc[i]) / jnp.maximum(jnp.abs(den), jnp.exp(-u['m_t']))
        u['out_ref'][0, :, u['h'] * ML_DV:(u['h'] + 1) * ML_DV] = hh
        b_last = u['b_col'][0:1] if u['d'] else u['b_col'][L - 1:L]
        logu_col = b_last - u['b_col'] + u['i_col']
        logu_row = b_last - u['b_row'] + u['i_row']
        u['m_new'] = jnp.maximum(b_last + u['m'], jnp.max(logu_row, axis=1, keepdims=True))
        u['decay'] = jnp.exp(b_last + u['m'] - u['m_new'])
        u['ku'] = u['k'] * jnp.exp(logu_col - u['m_new'])
    kv = [_dot_tn(u['ku'], u['v']) for u in units]
    for i, u in enumerate(units):
        d, h = u['d'], u['h']
        c_ref[d, h] = u['decay'] * u['c'] + kv[i]
        n_ref[d, h] = u['decay'] * u['n'] + jnp.sum(u['ku'], axis=0, keepdims=True)
        m_ref[d, h] = u['m_new']


def _mlstm(proj3, gates_t, bias_row, bias_col):
    B, T, _ = proj3.shape
    L = ML_CHUNK
    nc = T // L

    def specs(cm):
        return [pl.BlockSpec((1, L, ML_QK_W), lambda b, c: (b, cm(c), 0)),
                pl.BlockSpec((1, L, ML_QK_W), lambda b, c: (b, cm(c), 1)),
                pl.BlockSpec((1, L, ML_V_W), lambda b, c: (b, cm(c), 1)),
                pl.BlockSpec((1, L, LANES), lambda b, c: (b, cm(c), (8 * D_MODEL) // LANES)),
                pl.BlockSpec((1, 16, L), lambda b, c: (b, 0, cm(c)))]

    fwd = lambda c: c
    bwd = lambda c: nc - 1 - c
    out = jax.ShapeDtypeStruct((B, T, ML_V_W), f32)
    return pl.pallas_call(
        _mlstm_body,
        grid=(B, nc),
        in_specs=[pl.BlockSpec((1, LANES), lambda b, c: (0, 0)), pl.BlockSpec((16, 1), lambda b, c: (0, 0))]
        + specs(fwd) + specs(bwd),
        out_specs=[pl.BlockSpec((1, L, ML_V_W), lambda b, c: (b, c, 0)),
                   pl.BlockSpec((1, L, ML_V_W), lambda b, c: (b, bwd(c), 0))],
        out_shape=[out, out],
        scratch_shapes=[pltpu.VMEM((2, ML_HEADS, ML_DQK, ML_DV), f32),
                        pltpu.VMEM((2, ML_HEADS, 1, ML_DQK), f32),
                        pltpu.VMEM((2, ML_HEADS, 1, 1), f32)],
        compiler_params=_cparams(("parallel", "arbitrary")),
        name="mlstm",
    )(bias_row, bias_col, *([proj3] * 4), gates_t, *([proj3] * 4), gates_t)


def _shift(p, prev8, next8, mu, first, last):
    tt = p.shape[0]
    rows = lax.broadcasted_iota(i32, p.shape, 0)
    prev_row = jnp.where(first, 0.0, prev8[SUBLANES - 1:SUBLANES])
    next_row = jnp.where(last, 0.0, next8[0:1])
    pm = jnp.where(rows == 0, prev_row, pltpu.roll(p, 1, axis=0))
    nx = jnp.where(rows == tt - 1, next_row, pltpu.roll(p, tt - 1, axis=0))
    return p + mu * (0.5 * (pm + nx) - p)


def _rw_prep_body(r_ref, rp_ref, rn_ref, k_ref, kp_ref, kn_ref, v_ref, vp_ref, vn_ref,
                  l_ref, lp_ref, ln_ref, mur_ref, muk_ref, muv_ref, mul_ref,
                  w0_ref, w2h_ref, w2l_ref, a0_ref, a2_ref, g2_ref,
                  ro_ref, ko_ref, vo_ref, ao_ref, lwf_ref, lwb_ref, go_ref):
    i = pl.program_id(1)
    first = i == 0
    last = i == pl.num_programs(1) - 1
    lo = _shift(l_ref[0], lp_ref[0], ln_ref[0], mul_ref[...], first, last)
    ad = lo[:, 256:320]
    sg = _sigmoid(lo[:, 320:448])
    th = [jnp.tanh(lo[:, 128:192]), jnp.tanh(lo[:, 192:256])]
    th_hi = [t.astype(bf16) for t in th]
    th_lo = [(t - h.astype(f32)).astype(bf16) for t, h in zip(th, th_hi)]

    def mm(a, b):
        return jnp.dot(a, b, preferred_element_type=f32)

    CW = 2 * LANES
    for cb in range(RW_W // CW):
        sl = slice(cb * CW, (cb + 1) * CW)
        cols = []
        for src, prv, nxt, mu in ((r_ref, rp_ref, rn_ref, mur_ref), (k_ref, kp_ref, kn_ref, muk_ref),
                                  (v_ref, vp_ref, vn_ref, muv_ref)):
            cols.append(_shift(src[0, :, sl], prv[0, :, sl], nxt[0, :, sl], mu[:, sl], first, last))
        lws = []
        for d in range(2):
            wl = w0_ref[d:d + 1, sl] + (mm(th_hi[d], w2h_ref[d, :, sl]) + mm(th_hi[d], w2l_ref[d, :, sl])
                                        + mm(th_lo[d], w2h_ref[d, :, sl]))
            lws.append(-jnp.exp(-_softplus(-wl) - 0.5))
        a = _sigmoid(a0_ref[:, sl] + _dot(ad, a2_ref[:, sl]))
        go_ref[0, :, sl] = _dot(sg, g2_ref[:, sl])
        for j in range(2):
            hp = 2 * cb + j
            half = slice(j * LANES, (j + 1) * LANES)
            ro_ref[0, hp] = cols[0][:, half]
            ko_ref[0, hp] = cols[1][:, half]
            vo_ref[0, hp] = cols[2][:, half]
            ao_ref[0, hp] = a[:, half]
            lwf_ref[0, hp] = lws[0][:, half]
            lwb_ref[0, hp] = lws[1][:, half]


def _rw_prep(proj3, mu_r, mu_k, mu_v, mu_l, w0, w2h, w2l, a0, a2, g2):
    B, T, _ = proj3.shape
    tt = min(256, T)
    nt = T // tt
    g8 = tt // SUBLANES
    n8 = T // SUBLANES

    def tile(width, blk):
        return [pl.BlockSpec((1, tt, width), lambda b, i: (b, i, blk)),
                pl.BlockSpec((1, SUBLANES, width), lambda b, i: (b, jnp.maximum(i * g8 - 1, 0), blk)),
                pl.BlockSpec((1, SUBLANES, width), lambda b, i: (b, jnp.minimum((i + 1) * g8, n8 - 1), blk))]

    full = lambda shape: pl.BlockSpec(shape, lambda b, i: (0,) * len(shape))
    in_specs = (tile(D_MODEL, 3) + tile(D_MODEL, 4) + tile(D_MODEL, 5) + tile(512, SMALL_BLK)
                + [full((1, RW_W))] * 3 + [full((1, 512)), full((2, RW_W)), full((2, 64, RW_W)), full((2, 64, RW_W)),
                                           full((1, RW_W)), full((64, RW_W)), full((128, RW_W))])
    hm = jax.ShapeDtypeStruct((B, RW_PAIRS, T, LANES), f32)
    hm_spec = pl.BlockSpec((1, RW_PAIRS, tt, LANES), lambda b, i: (b, 0, i, 0))
    return pl.pallas_call(
        _rw_prep_body,
        grid=(B, nt),
        in_specs=in_specs,
        out_specs=[hm_spec] * 6 + [pl.BlockSpec((1, tt, RW_W), lambda b, i: (b, i, 0))],
        out_shape=[hm] * 6 + [jax.ShapeDtypeStruct((B, T, RW_W), f32)],
        compiler_params=_cparams(("parallel", "parallel")),
        name="rw_prep",
    )(*([proj3] * 12), mu_r, mu_k, mu_v, mu_l, w0, w2h, w2l, a0, a2, g2)


def _chunk_cumsum(x, reverse):
    L = x.shape[0]
    rows = lax.broadcasted_iota(i32, x.shape, 0)
    s = 1
    while s < L:
        if reverse:
            x = x + jnp.where(rows < L - s, pltpu.roll(x, L - s, axis=0), 0.0)
        else:
            x = x + jnp.where(rows >= s, pltpu.roll(x, s, axis=0), 0.0)
        s *= 2
    return x


def _block_diag(x, lo):
    return jnp.concatenate([jnp.where(lo, x, 0.0), jnp.where(lo, 0.0, x)], axis=0)


def _rw_prepare(units, lo):
    L = units[0][0].shape[0]
    prepared = []
    for r, kp, v, kk, bh, lw, cum, reverse in units:
        tot = cum[0:1] if reverse else cum[L - 1:L]
        gi = jnp.exp(-cum)
        gl = jnp.exp(tot - cum)
        lhs = jnp.concatenate([-kk * jnp.exp(cum - lw), r * jnp.exp(cum)], axis=0)
        rhs = jnp.concatenate([_block_diag(bh * gi, lo), _block_diag(kp * gi, lo)], axis=0)
        tail = jnp.concatenate([bh * gl, kp * gl], axis=0)
        prepared.append((lhs.astype(bf16), rhs.astype(bf16), tail.astype(bf16),
                         _block_diag(v, lo).astype(bf16), v.astype(bf16), jnp.exp(tot)))
    return prepared


def _rw_solve(prepared, states, reverses, rowi, coli, lo):
    n = len(prepared)
    L = prepared[0][4].shape[0]
    G = [_dot_nt(prepared[i][0], prepared[i][1]) for i in range(n)]
    tok = yield None
    sv = [_dot_nt(prepared[i][0], states[i]) for i in range(n)]
    sv[0] = _anchor(sv[0], tok)
    tok = yield None
    p, a_rb, low = [], [], []
    for i in range(n):
        strict = (coli > rowi) if reverses[i] else (coli < rowi)
        incl = (coli >= rowi) if reverses[i] else (coli <= rowi)
        p.append(jnp.where(strict, G[i][:L, :2 * L], 0.0))
        a_rb.append(jnp.where(incl, G[i][L:, :2 * L], 0.0))
        low.append(jnp.concatenate([jnp.where(strict, G[i][:L, 2 * L:], 0.0),
                                    jnp.where(incl, G[i][L:, 2 * L:], 0.0)], axis=0))
    p[0] = _anchor(p[0], tok)
    av = [_dot(low[i], prepared[i][3]) for i in range(n)]
    tok = yield None
    y = [sv[i][:L] + av[i][:L] for i in range(n)]
    y[0] = _anchor(y[0], tok)
    s = 1
    while 2 * s < L:
        both = [_dot(p[i], jnp.concatenate([_block_diag(y[i], lo), _block_diag(p[i], lo)], axis=1))
                for i in range(n)]
        y = [y[i] + both[i][:, :2 * L] for i in range(n)]
        p = [both[i][:, 2 * L:] for i in range(n)]
        y[0] = _anchor(y[0], (yield None))
        s *= 2
    y = [y[i] + _dot(p[i], _block_diag(y[i], lo)) for i in range(n)]
    y[0] = _anchor(y[0], (yield None))
    ru = [_dot(a_rb[i], _block_diag(y[i], lo)) for i in range(n)]
    y[0] = _anchor(y[0], (yield None))
    upd = [_dot_tn(jnp.concatenate([y[i].astype(bf16), prepared[i][4]], axis=0), prepared[i][2])
           for i in range(n)]
    outs = [sv[i][L:] + av[i][L:] + ru[i] for i in range(n)]
    new_states = [states[i] * prepared[i][5] + upd[i] for i in range(n)]
    yield outs, new_states


def _zero_of(x):
    u = pltpu.bitcast(x[0:SUBLANES].astype(f32), jnp.uint32)
    z = lax.shift_right_logical(lax.shift_right_logical(u, jnp.uint32(16)), jnp.uint32(16))
    return pltpu.bitcast(z, f32)[0:1]


def _anchor(x, tok):
    return x if tok is None else x + tok


def _rw_scan_body(rf_ref, kf_ref, vf_ref, af_ref, lwf_ref, rb_ref, kb_ref, vb_ref, ab_ref, lwb_ref,
                  kk_ref, ka_ref, of_ref, ob_ref, s_ref, *slots):
    L = RW_CHUNK
    H = RW_HEAD
    i = pl.program_id(1)
    names = 6
    slot_a, slot_b = slots[:names], slots[names:]

    @pl.when(i == 0)
    def _():
        s_ref[...] = jnp.zeros_like(s_ref)
        for ref in slots:
            ref[...] = jnp.zeros_like(ref)

    rowi = lax.broadcasted_iota(i32, (L, 2 * L), 0)
    coli = lax.broadcasted_iota(i32, (L, 2 * L), 1) % L
    lo = lax.broadcasted_iota(i32, (1, LANES), 1) < H
    diag = (lax.broadcasted_iota(i32, (LANES, LANES), 0) < H) == (lax.broadcasted_iota(i32, (LANES, LANES), 1) < H)
    dirs = ((rf_ref, kf_ref, vf_ref, af_ref, lwf_ref), (rb_ref, kb_ref, vb_ref, ab_ref, lwb_ref))

    def prepare_pair(hp, dst):
        units = []
        k_k = kk_ref[hp]
        k_a = ka_ref[hp]
        for d, (r_ref, k_ref, v_ref, a_ref, lw_ref) in enumerate(dirs):
            k = k_ref[0, hp]
            a = a_ref[0, hp]
            lw = lw_ref[0, hp]
            cum = _chunk_cumsum(lw, d == 1)
            kk = k * k_k
            sq = kk * kk
            n0 = jnp.sum(jnp.where(lo, sq, 0.0), axis=-1, keepdims=True)
            n1 = jnp.sum(jnp.where(lo, 0.0, sq), axis=-1, keepdims=True)
            kk = kk / jnp.maximum(jnp.sqrt(jnp.where(lo, n0, n1)), L2_EPS)
            kp = k * (1.0 + (a - 1.0) * k_a)
            units.append((r_ref[0, hp], kp, v_ref[0, hp], kk, kk * a, lw, cum, d == 1))
        tok = None
        for d, ops in enumerate(_rw_prepare(units, lo)):
            for ref, val in zip(dst, ops):
                ref[2 * hp + d] = val
            tok = _zero_of(ops[2]) if tok is None else tok + _zero_of(ops[2])
        return tok

    def step(src, dst):
        prepared, states, reverses = [], [], []
        for hp in range(RW_PAIRS):
            for d in range(2):
                prepared.append(tuple(ref[2 * hp + d] for ref in src))
                states.append(s_ref[d, hp])
                reverses.append(d == 1)
        todo = list(range(RW_PAIRS))
        solve = _rw_solve(prepared, states, reverses, rowi, coli, lo)
        result = next(solve)
        while result is None:
            result = solve.send(prepare_pair(todo.pop(0), dst) if todo else None)
        for hp in todo:
            prepare_pair(hp, dst)
        outs, new_states = result
        for hp in range(RW_PAIRS):
            for d, o_ref in enumerate((of_ref, ob_ref)):
                o_ref[0, hp] = outs[2 * hp + d]
                s_ref[d, hp] = jnp.where(diag, new_states[2 * hp + d], 0.0)

    @pl.when(i % 2 == 0)
    def _():
        step(slot_a, slot_b)

    @pl.when(i % 2 == 1)
    def _():
        step(slot_b, slot_a)


def _rw_scan(r, k, v, a, lwf, lwb, k_k, k_a):
    B, _, T, _ = r.shape
    L = RW_CHUNK
    nc = T // L
    last = nc - 1
    fin = pl.BlockSpec((1, RW_PAIRS, L, LANES), lambda b, i: (b, 0, jnp.minimum(i, last), 0))
    bin_ = pl.BlockSpec((1, RW_PAIRS, L, LANES), lambda b, i: (b, 0, last - jnp.minimum(i, last), 0))
    fout = pl.BlockSpec((1, RW_PAIRS, L, LANES), lambda b, i: (b, 0, jnp.maximum(i - 1, 0), 0))
    bout = pl.BlockSpec((1, RW_PAIRS, L, LANES), lambda b, i: (b, 0, last - jnp.maximum(i - 1, 0), 0))
    ps = pl.BlockSpec((RW_PAIRS, 1, LANES), lambda b, i: (0, 0, 0))
    hm = jax.ShapeDtypeStruct((B, RW_PAIRS, T, LANES), f32)
    nu = 2 * RW_PAIRS
    slot = [pltpu.VMEM((nu, 2 * L, LANES), bf16), pltpu.VMEM((nu, 4 * L, LANES), bf16),
            pltpu.VMEM((nu, 2 * L, LANES), bf16), pltpu.VMEM((nu, 2 * L, LANES), bf16),
            pltpu.VMEM((nu, L, LANES), bf16), pltpu.VMEM((nu, 1, LANES), f32)]
    return pl.pallas_call(
        _rw_scan_body,
        grid=(B, nc + 1),
        in_specs=[fin] * 5 + [bin_] * 5 + [ps, ps],
        out_specs=[fout, bout],
        out_shape=[hm, hm],
        scratch_shapes=[pltpu.VMEM((2, RW_PAIRS, LANES, LANES), f32)] + slot + slot,
        compiler_params=_cparams(("parallel", "arbitrary")),
        name="rw_scan",
    )(r, k, v, a, lwf, r, k, v, a, lwb, k_k, k_a)


def _head_mean(x, avg):
    hi = x.astype(bf16)
    lo = (x - hi.astype(f32)).astype(bf16)
    return (jnp.dot(hi, avg, preferred_element_type=f32) + jnp.dot(lo, avg, preferred_element_type=f32))


def _rw_post_body(of_ref, ob_ref, r_ref, k_ref, v_ref, a_ref, g_ref, ka_ref, rk_ref, gg_ref, gb_ref, o_ref):
    same_head = (lax.broadcasted_iota(i32, (LANES, LANES), 0) < RW_HEAD) == (
        lax.broadcasted_iota(i32, (LANES, LANES), 1) < RW_HEAD)
    avg = jnp.where(same_head, 1.0 / RW_HEAD, 0.0).astype(bf16)
    for hp in range(RW_PAIRS):
        out = of_ref[0, hp] + ob_ref[0, hp]
        kp = k_ref[0, hp] * (1.0 + (a_ref[0, hp] - 1.0) * ka_ref[hp])
        rk = r_ref[0, hp] * kp * rk_ref[hp]
        cen = out - _head_mean(out, avg)
        var = _head_mean(cen * cen, avg)
        normed = cen * lax.rsqrt(var + RW_GN_EPS)
        bonus = (_head_mean(rk, avg) * RW_HEAD) * v_ref[0, hp]
        lanes = slice(hp * LANES, (hp + 1) * LANES)
        o_ref[0, :, lanes] = (normed * gg_ref[hp] + gb_ref[hp] + bonus) * g_ref[0, :, lanes]


def _rw_post(out_f, out_b, r, k, v, a, g, k_a, r_k, gn_g, gn_b):
    B, _, T, _ = r.shape
    tt = min(256, T)
    hs = pl.BlockSpec((1, RW_PAIRS, tt, LANES), lambda b, i: (b, 0, i, 0))
    ps = pl.BlockSpec((RW_PAIRS, 1, LANES), lambda b, i: (0, 0, 0))
    ts = pl.BlockSpec((1, tt, RW_W), lambda b, i: (b, i, 0))
    return pl.pallas_call(
        _rw_post_body,
        grid=(B, T // tt),
        in_specs=[hs] * 6 + [ts] + [ps] * 4,
        out_specs=ts,
        out_shape=jax.ShapeDtypeStruct((B, T, RW_W), f32),
        compiler_params=_cparams(("parallel", "parallel")),
        name="rw_post",
    )(out_f, out_b, r, k, v, a, g, k_a, r_k, gn_g, gn_b)


def _merge_body(hf_ref, hb_ref, o_ref, ng_ref, rw_ref, ga_ref, gb_ref, x_ref, wa_ref, wb_ref, wo_ref, nf_ref,
                wr_ref, br_ref, x1_ref, xn_ref, lg_ref):
    heads = []
    for h in range(ML_HEADS):
        sl = slice(h * ML_DV, (h + 1) * ML_DV)
        t = hf_ref[:, sl] + hb_ref[:, sl]
        t = t * lax.rsqrt(jnp.mean(t * t, axis=-1, keepdims=True) + NORM_EPS)
        heads.append((_sigmoid(o_ref[:, sl]) * (t * ng_ref[:, sl])).astype(bf16))
    ml = jnp.concatenate(heads, axis=1)
    y_a = jnp.dot(ml, wa_ref[...], preferred_element_type=f32)
    y_b = jnp.dot(rw_ref[...].astype(bf16), wb_ref[...], preferred_element_type=f32)
    merged = _sigmoid(ga_ref[...]) * y_a + _sigmoid(gb_ref[...]) * y_b
    x1 = x_ref[...] + jnp.dot(merged.astype(bf16), wo_ref[...], preferred_element_type=f32)
    x1_ref[...] = x1
    xn = x1 * lax.rsqrt(jnp.mean(x1 * x1, axis=-1, keepdims=True) + NORM_EPS) * nf_ref[...]
    xn_ref[...] = xn
    wr = wr_ref[...]
    wr_hi = wr.astype(bf16)
    wr_lo = (wr - wr_hi.astype(f32)).astype(bf16)
    xn_hi = xn.astype(bf16)
    xn_lo = (xn - xn_hi.astype(f32)).astype(bf16)
    lg_ref[...] = (jnp.dot(xn_hi, wr_hi, preferred_element_type=f32)
                   + jnp.dot(xn_hi, wr_lo, preferred_element_type=f32)
                   + jnp.dot(xn_lo, wr_hi, preferred_element_type=f32)) + br_ref[...]


def _merge(h_f, h_b, ml_norm_g, rw_out, proj, x2d, w_up_a, w_up_b, w_out, norm_ffn, w_router, b_router):
    n = x2d.shape[0]
    tm = min(512, n)
    row = lambda blk: pl.BlockSpec((tm, D_MODEL), lambda i: (i, blk))
    full = lambda shape: pl.BlockSpec(shape, lambda i: (0,) * len(shape))
    return pl.pallas_call(
        _merge_body,
        grid=(n // tm,),
        in_specs=[row(0), row(0), row(2), full((1, ML_V_W)), row(0), row(6), row(7), row(0),
                  full((D_MODEL, D_MODEL)), full((D_MODEL, D_MODEL)), full((D_MODEL, D_MODEL)),
                  full((1, D_MODEL)), full((D_MODEL, LANES)), full((1, LANES))],
        out_specs=[row(0), row(0), pl.BlockSpec((tm, LANES), lambda i: (i, 0))],
        out_shape=[jax.ShapeDtypeStruct((n, D_MODEL), f32), jax.ShapeDtypeStruct((n, D_MODEL), f32),
                   jax.ShapeDtypeStruct((n, LANES), f32)],
        compiler_params=_cparams(("parallel",)),
        name="merge",
    )(h_f, h_b, proj, ml_norm_g, rw_out, proj, proj, x2d, w_up_a, w_up_b, w_out, norm_ffn, w_router, b_router)


def _first_argmax(p, width):
    lane = lax.broadcasted_iota(i32, p.shape, 1)
    top = jnp.max(p, axis=-1, keepdims=True)
    idx = jnp.min(jnp.where(p == top, lane, width), axis=-1, keepdims=True)
    return top, idx, lane


def _route_body(lg_ref, meta_ref, gate_ref, cnt_ref, carry_ref):
    @pl.when(pl.program_id(0) == 0)
    def _():
        carry_ref[...] = jnp.zeros_like(carry_ref)

    lg = lg_ref[...]
    tr = lg.shape[0]
    gl = lg[:, 0:N_GROUPS]
    ge = jnp.exp(gl - jnp.max(gl, axis=-1, keepdims=True))
    p_group = ge / jnp.sum(ge, axis=-1, keepdims=True)
    p_g, g_idx, _ = _first_argmax(p_group, N_GROUPS)
    el = jnp.zeros((tr, EXPERTS_PER_GROUP), f32)
    for g in range(N_GROUPS):
        lo = N_GROUPS + g * EXPERTS_PER_GROUP
        el = el + jnp.where(g_idx == g, 1.0, 0.0) * lg[:, lo:lo + EXPERTS_PER_GROUP]
    ee = jnp.exp(el - jnp.max(el, axis=-1, keepdims=True))
    p_exp = ee / jnp.sum(ee, axis=-1, keepdims=True)
    p1, i1, lane8 = _first_argmax(p_exp, EXPERTS_PER_GROUP)
    p2, i2, _ = _first_argmax(jnp.where(lane8 == i1, -1.0, p_exp), EXPERTS_PER_GROUP)
    psum = p1 + p2
    g1 = p_g * p1 / psum
    g2 = p_g * p2 / psum
    e0 = g_idx * EXPERTS_PER_GROUP + i1
    e1 = g_idx * EXPERTS_PER_GROUP + i2
    lane = lax.broadcasted_iota(i32, (tr, LANES), 1)
    hit0 = lane == e0
    hit1 = lane == e1
    onehot = jnp.where(hit0 | hit1, 1.0, 0.0)
    rowi = lax.broadcasted_iota(i32, (tr, tr), 0)
    coli = lax.broadcasted_iota(i32, (tr, tr), 1)
    before = jnp.where(coli < rowi, 1.0, 0.0)
    prefix = _dot(before, onehot) + carry_ref[...]
    r0 = jnp.sum(jnp.where(hit0, prefix, 0.0), axis=-1, keepdims=True).astype(i32)
    r1 = jnp.sum(jnp.where(hit1, prefix, 0.0), axis=-1, keepdims=True).astype(i32)
    carry = carry_ref[...] + jnp.sum(onehot, axis=0, keepdims=True)
    carry_ref[...] = carry
    cnt_ref[...] = carry.astype(i32)
    meta_ref[...] = jnp.where(lane == 0, e0, jnp.where(lane == 1, e1, jnp.where(lane == 2, r0, r1)))
    gate_ref[...] = jnp.where(lane == 0, g1, g2)


def _route(logits):
    n = logits.shape[0]
    tr = min(512, n)
    return pl.pallas_call(
        _route_body,
        grid=(n // tr,),
        in_specs=[pl.BlockSpec((tr, LANES), lambda i: (i, 0))],
        out_specs=[pl.BlockSpec((tr, LANES), lambda i: (i, 0)), pl.BlockSpec((tr, LANES), lambda i: (i, 0)),
                   pl.BlockSpec((1, LANES), lambda i: (0, 0))],
        out_shape=[jax.ShapeDtypeStruct((n, LANES), i32), jax.ShapeDtypeStruct((n, LANES), f32),
                   jax.ShapeDtypeStruct((1, LANES), i32)],
        scratch_shapes=[pltpu.VMEM((1, LANES), f32)],
        compiler_params=_cparams(("arbitrary",)),
        name="route",
    )(logits)


def _dest_body(meta_ref, ps_ref, o_ref):
    meta = meta_ref[...]
    lane = lax.broadcasted_iota(i32, meta.shape, 1)
    ps = ps_ref[...].astype(f32)
    seg0 = jnp.sum(jnp.where(lane == meta[:, 0:1], ps, 0.0), axis=-1, keepdims=True).astype(i32)
    seg1 = jnp.sum(jnp.where(lane == meta[:, 1:2], ps, 0.0), axis=-1, keepdims=True).astype(i32)
    o_ref[...] = jnp.where(lane == 0, seg0 + meta[:, 2:3], seg1 + meta[:, 3:4])


def _dest(meta, pstart_row):
    n = meta.shape[0]
    tr = min(1024, n)
    return pl.pallas_call(
        _dest_body,
        grid=(n // tr,),
        in_specs=[pl.BlockSpec((tr, LANES), lambda i: (i, 0)), pl.BlockSpec((1, LANES), lambda i: (0, 0))],
        out_specs=pl.BlockSpec((tr, LANES), lambda i: (i, 0)),
        out_shape=jax.ShapeDtypeStruct((n, LANES), i32),
        compiler_params=_cparams(("parallel",)),
        name="moe_dest",
    )(meta, pstart_row)


def _scatter_body(d0_ref, d1_ref, x_ref, buf_in, xs_hbm, sem):
    del buf_in
    ts = d0_ref.shape[0]

    def row_copy(t, dst):
        return pltpu.make_async_copy(x_ref.at[pl.ds(t, 1)], xs_hbm.at[pl.ds(dst, 1)], sem)

    def issue(t, c):
        row_copy(t, d0_ref[t]).start()
        row_copy(t, d1_ref[t]).start()
        return c

    lax.fori_loop(0, ts, issue, 0, unroll=DMA_UNROLL)

    def drain(t, c):
        row_copy(0, 0).wait()
        row_copy(0, 0).wait()
        return c

    lax.fori_loop(0, ts, drain, 0, unroll=DMA_UNROLL)


def _scatter(xn, d0, d1, n_rows):
    n = xn.shape[0]
    ts = min(512, n)
    tok = pl.BlockSpec((ts,), lambda i: (i,), memory_space=pltpu.SMEM)
    anyspec = pl.BlockSpec(memory_space=pl.ANY)
    buf = jnp.zeros((n_rows, D_MODEL), f32)
    return pl.pallas_call(
        _scatter_body,
        grid=(n // ts,),
        in_specs=[tok, tok, pl.BlockSpec((ts, D_MODEL), lambda i: (i, 0)), anyspec],
        out_specs=anyspec,
        scratch_shapes=[pltpu.SemaphoreType.DMA(())],
        out_shape=jax.ShapeDtypeStruct((n_rows, D_MODEL), f32),
        input_output_aliases={3: 0},
        compiler_params=_cparams(("arbitrary",)),
        name="moe_scatter",
    )(d0, d1, xn, buf)


def _experts_body(be_ref, nu_ref, xs_ref, wg_ref, wu_ref, wd_ref, ys_ref):
    del be_ref

    @pl.when(pl.program_id(0) < nu_ref[0])
    def _():
        xb = xs_ref[...].astype(bf16)
        hg = jnp.dot(xb, wg_ref[0], preferred_element_type=f32)
        hu = jnp.dot(xb, wu_ref[0], preferred_element_type=f32)
        hb = hg * _sigmoid(hg) * hu
        ys_ref[...] = jnp.dot(hb.astype(bf16), wd_ref[0], preferred_element_type=f32)

    @pl.when(pl.program_id(0) >= nu_ref[0])
    def _():
        ys_ref[...] = jnp.zeros_like(ys_ref)


def _experts(xs, block_e, n_used, w_gate, w_up, w_down):
    n_rows = xs.shape[0]
    nb = n_rows // MOE_BLOCK
    return pl.pallas_call(
        _experts_body,
        grid_spec=pltpu.PrefetchScalarGridSpec(
            num_scalar_prefetch=2, grid=(nb,),
            in_specs=[pl.BlockSpec((MOE_BLOCK, D_MODEL), lambda i, be, nu: (i, 0)),
                      pl.BlockSpec((1, D_MODEL, D_EXPERT), lambda i, be, nu: (be[i], 0, 0)),
                      pl.BlockSpec((1, D_MODEL, D_EXPERT), lambda i, be, nu: (be[i], 0, 0)),
                      pl.BlockSpec((1, D_EXPERT, D_MODEL), lambda i, be, nu: (be[i], 0, 0))],
            out_specs=pl.BlockSpec((MOE_BLOCK, D_MODEL), lambda i, be, nu: (i, 0))),
        out_shape=jax.ShapeDtypeStruct((n_rows, D_MODEL), f32),
        compiler_params=_cparams(("arbitrary",)),
        name="moe_experts",
    )(block_e, n_used, xs, w_gate, w_up, w_down)


def _combine_body(d0_ref, d1_ref, x1_ref, gate_ref, nf_ref, ys_hbm, o_ref, y0_ref, y1_ref, sem):
    tc = d0_ref.shape[0]

    def row_copy(src, dst_ref, t):
        return pltpu.make_async_copy(ys_hbm.at[pl.ds(src, 1)], dst_ref.at[pl.ds(t, 1)], sem)

    def issue(t, c):
        row_copy(d0_ref[t], y0_ref, t).start()
        row_copy(d1_ref[t], y1_ref, t).start()
        return c

    lax.fori_loop(0, tc, issue, 0, unroll=DMA_UNROLL)

    def drain(t, c):
        row_copy(0, y0_ref, 0).wait()
        row_copy(0, y1_ref, 0).wait()
        return c

    lax.fori_loop(0, tc, drain, 0, unroll=DMA_UNROLL)
    gates = gate_ref[...]
    x2 = x1_ref[...] + gates[:, 0:1] * y0_ref[...] + gates[:, 1:2] * y1_ref[...]
    o_ref[...] = x2 * lax.rsqrt(jnp.mean(x2 * x2, axis=-1, keepdims=True) + NORM_EPS) * nf_ref[...]


def _combine(x1, gates, norm_final, ys, d0, d1):
    n = x1.shape[0]
    tc = min(256, n)
    tok = pl.BlockSpec((tc,), lambda i: (i,), memory_space=pltpu.SMEM)
    return pl.pallas_call(
        _combine_body,
        grid=(n // tc,),
        in_specs=[tok, tok,
                  pl.BlockSpec((tc, D_MODEL), lambda i: (i, 0)),
                  pl.BlockSpec((tc, LANES), lambda i: (i, 0)),
                  pl.BlockSpec((1, D_MODEL), lambda i: (0, 0)),
                  pl.BlockSpec(memory_space=pl.ANY)],
        out_specs=pl.BlockSpec((tc, D_MODEL), lambda i: (i, 0)),
        scratch_shapes=[pltpu.VMEM((tc, D_MODEL), f32), pltpu.VMEM((tc, D_MODEL), f32),
                        pltpu.SemaphoreType.DMA(())],
        out_shape=jax.ShapeDtypeStruct((n, D_MODEL), f32),
        compiler_params=_cparams(("arbitrary",)),
        name="moe_combine",
    )(d0, d1, x1, gates, norm_final, ys)


def _prep_weights(w_in, rw_mu, w_router_group, b_router_group, w_router_expert, b_router_expert):
    rw0 = ML_COLS
    g0 = ML_COLS + RW_COLS
    zeros = lambda c: jnp.zeros((D_MODEL, c), w_in.dtype)
    w_all = jnp.concatenate([
        w_in[:, 0:3072],
        w_in[:, rw0:rw0 + 3072],
        w_in[:, g0:g0 + 2048],
        w_in[:, 3072:3088], zeros(112),
        w_in[:, rw0 + 3072:rw0 + 3392], zeros(64)], axis=1).astype(bf16)
    mu_r = rw_mu[None, 0:1024]
    mu_k = rw_mu[None, 1024:2048]
    mu_v = rw_mu[None, 2048:3072]
    mu_l = jnp.concatenate([jnp.zeros((128,), f32), rw_mu[3072:3392], jnp.zeros((64,), f32)])[None]
    w_router = jnp.concatenate([w_router_group, w_router_expert,
                                jnp.zeros((D_MODEL, LANES - N_GROUPS - N_EXPERTS), f32)], axis=1)
    b_router = jnp.concatenate([b_router_group, b_router_expert,
                                jnp.zeros((LANES - N_GROUPS - N_EXPERTS,), f32)])[None]
    return w_all, mu_r, mu_k, mu_v, mu_l, w_router, b_router


def _pairs(p):
    return p.reshape(RW_PAIRS, 1, LANES)


def _forward(x, norm_mix, w_all, mus, ml_b_i, ml_b_f, ml_norm_g, rw_w0, rw_w2, rw_a0, rw_a2, rw_g2,
             rw_k_k, rw_k_a, rw_r_k, rw_gn_g, rw_gn_b, w_up_a, w_up_b, w_out, norm_ffn,
             w_router, b_router, w_gate, w_up, w_down, norm_final):
    B, T, _ = x.shape
    n = B * T
    x2d = x.reshape(n, D_MODEL)
    proj = _proj(x2d, norm_mix[None], w_all)
    proj3 = proj.reshape(B, T, W_ALL)
    gates_t = jnp.swapaxes(proj3[:, :, 8 * D_MODEL:8 * D_MODEL + 16], 1, 2)
    bias16 = jnp.concatenate([ml_b_i.reshape(-1), ml_b_f.reshape(-1)])
    bias_row = jnp.concatenate([bias16, jnp.zeros((LANES - 16,), f32)])[None]
    h_f, h_b = _mlstm(proj3, gates_t, bias_row, bias16[:, None])
    mu_r, mu_k, mu_v, mu_l = mus
    w2_hi = rw_w2.astype(bf16)
    w2_lo = (rw_w2 - w2_hi.astype(f32)).astype(bf16)
    r, k, v, a, lwf, lwb, g = _rw_prep(proj3, mu_r, mu_k, mu_v, mu_l, rw_w0, w2_hi, w2_lo, rw_a0[None],
                                       rw_a2, rw_g2)
    out_f, out_b = _rw_scan(r, k, v, a, lwf, lwb, _pairs(rw_k_k), _pairs(rw_k_a))
    rw_out = _rw_post(out_f, out_b, r, k, v, a, g, _pairs(rw_k_a), _pairs(rw_r_k), _pairs(rw_gn_g),
                      _pairs(rw_gn_b))
    x1, xn, logits = _merge(h_f.reshape(n, ML_V_W), h_b.reshape(n, ML_V_W), ml_norm_g[None],
                            rw_out.reshape(n, RW_W), proj, x2d,
                            w_up_a, w_up_b, w_out, norm_ffn[None], w_router, b_router)
    meta, gates, counts = _route(logits)
    counts = counts[0, :N_EXPERTS]
    padded = ((counts + MOE_BLOCK - 1) // MOE_BLOCK) * MOE_BLOCK
    pend = jnp.cumsum(padded)
    pstart = (pend - padded).astype(i32)
    nb = (2 * n) // MOE_BLOCK + N_EXPERTS
    block_start = jnp.arange(nb, dtype=i32) * MOE_BLOCK
    block_e = jnp.minimum(jnp.sum(block_start[:, None] >= pend[None, :], axis=1), N_EXPERTS - 1).astype(i32)
    n_used = (pend[-1:] // MOE_BLOCK).astype(i32)
    pstart_row = jnp.concatenate([pstart, jnp.zeros((LANES - N_EXPERTS,), i32)])[None]
    dest = _dest(meta, pstart_row)
    d0, d1 = dest[:, 0], dest[:, 1]
    xs = _scatter(xn, d0, d1, nb * MOE_BLOCK)
    ys = _experts(xs, block_e, n_used, w_gate, w_up, w_down)
    y = _combine(x1, gates, norm_final[None], ys, d0, d1)
    return y.reshape(B, T, D_MODEL)


def kernel(x_prompt, x_sample, norm_mix, w_in, ml_b_i, ml_b_f, ml_norm_g, rw_mu, rw_w0, rw_w2, rw_a0, rw_a2,
           rw_g2, rw_k_k, rw_k_a, rw_r_k, rw_gn_g, rw_gn_b, w_up_a, w_up_b, w_out, norm_ffn,
           w_router_group, b_router_group, w_router_expert, b_router_expert,
           w_expert_gate, w_expert_up, w_expert_down, norm_final):
    w_all, mu_r, mu_k, mu_v, mu_l, w_router, b_router = _prep_weights(
        w_in[0], rw_mu[0], w_router_group[0], b_router_group[0], w_router_expert[0], b_router_expert[0])
    args = (norm_mix[0], w_all, (mu_r, mu_k, mu_v, mu_l), ml_b_i[0], ml_b_f[0], ml_norm_g[0],
            rw_w0[0], rw_w2[0], rw_a0[0], rw_a2[0], rw_g2[0], rw_k_k[0], rw_k_a[0], rw_r_k[0].reshape(-1),
            rw_gn_g[0], rw_gn_b[0], w_up_a[0].astype(bf16), w_up_b[0].astype(bf16), w_out[0].astype(bf16),
            norm_ffn[0], w_router, b_router, w_expert_gate[0].astype(bf16), w_expert_up[0].astype(bf16),
            w_expert_down[0].astype(bf16), norm_final)
    return (_forward(x_prompt, *args), _forward(x_sample, *args))
```

```python
import functools

import jax
import jax.numpy as jnp
from jax import lax
from jax.experimental import pallas as pl
from jax.experimental.pallas import tpu as pltpu

f32 = jnp.float32
bf16 = jnp.bfloat16
i32 = jnp.int32

D_MODEL = 1024
ML_HEADS = 4
ML_DQK = 128
ML_DV = 256
ML_QK_W = ML_HEADS * ML_DQK
ML_V_W = ML_HEADS * ML_DV
ML_COLS = 2 * ML_QK_W + 2 * ML_V_W + 4 * ML_HEADS
RW_HEAD = 64
RW_HEADS = 16
RW_W = 1024
RW_PAIRS = RW_HEADS // 2
RW_COLS = 3 * RW_W + 64 + 64 + 64 + 128
N_GROUPS = 8
EXPERTS_PER_GROUP = 8
N_EXPERTS = 64
D_EXPERT = 512
NORM_EPS = 1e-6
RW_GN_EPS = 64e-5
L2_EPS = 1e-12

LANES = 128
SUBLANES = 8
W_ALL = 8 * D_MODEL + 512
SMALL_BLK = (8 * D_MODEL) // 512
ML_CHUNK = 128
RW_CHUNK = 64
MOE_BLOCK = 256
DMA_UNROLL = 8
VMEM_LIMIT = 56 * 1024 * 1024


def _cparams(sem):
    return pltpu.CompilerParams(dimension_semantics=sem, vmem_limit_bytes=VMEM_LIMIT)


def _dot(a, b):
    return jnp.dot(a.astype(bf16), b.astype(bf16), preferred_element_type=f32)


def _dot_nt(a, b):
    return lax.dot_general(a.astype(bf16), b.astype(bf16), (((1,), (1,)), ((), ())),
                           preferred_element_type=f32)


def _dot_tn(a, b):
    return lax.dot_general(a.astype(bf16), b.astype(bf16), (((0,), (0,)), ((), ())),
                           preferred_element_type=f32)


def _sigmoid(x):
    return 1.0 / (1.0 + jnp.exp(-x))


def _softplus(x):
    return jnp.maximum(x, 0.0) + jnp.log1p(jnp.exp(-jnp.abs(x)))


def _proj_body(x_ref, g_ref, w_ref, o_ref, xn_ref):
    @pl.when(pl.program_id(1) == 0)
    def _():
        x = x_ref[...]
        ms = jnp.mean(x * x, axis=-1, keepdims=True)
        xn_ref[...] = (x * lax.rsqrt(ms + NORM_EPS) * g_ref[...]).astype(bf16)

    o_ref[...] = jnp.dot(xn_ref[...], w_ref[...], preferred_element_type=f32)


def _proj(x2d, g, w_all):
    n = x2d.shape[0]
    tm = min(1024, n)
    tn = W_ALL // 4
    return pl.pallas_call(
        _proj_body,
        grid=(n // tm, W_ALL // tn),
        in_specs=[pl.BlockSpec((tm, D_MODEL), lambda i, j: (i, 0)),
                  pl.BlockSpec((1, D_MODEL), lambda i, j: (0, 0)),
                  pl.BlockSpec((D_MODEL, tn), lambda i, j: (0, j))],
        out_specs=pl.BlockSpec((tm, tn), lambda i, j: (i, j)),
        out_shape=jax.ShapeDtypeStruct((n, W_ALL), f32),
        scratch_shapes=[pltpu.VMEM((tm, D_MODEL), bf16)],
        compiler_params=_cparams(("parallel", "arbitrary")),
        name="proj",
    )(x2d, g, w_all)


def _log_sigmoid(x):
    return jnp.minimum(x, 0.0) - jnp.log1p(jnp.exp(-jnp.abs(x)))


def _mlstm_body(br_ref, bc_ref, qf_ref, kf_ref, vf_ref, gf_ref, gtf_ref,
                qb_ref, kb_ref, vb_ref, gb_ref, gtb_ref, hf_ref, hb_ref, c_ref, n_ref, m_ref):
    L = ML_CHUNK

    @pl.when(pl.program_id(1) == 0)
    def _():
        c_ref[...] = jnp.zeros_like(c_ref)
        n_ref[...] = jnp.zeros_like(n_ref)
        m_ref[...] = jnp.zeros_like(m_ref)

    rowi = lax.broadcasted_iota(i32, (L, L), 0)
    coli = lax.broadcasted_iota(i32, (L, L), 1)
    hp = lax.Precision.HIGHEST
    scale = ML_DQK ** -0.5
    units = []
    for d, (q_ref, k_ref, v_ref, g_ref, gt_ref, out_ref) in enumerate(
            ((qf_ref, kf_ref, vf_ref, gf_ref, gtf_ref, hf_ref), (qb_ref, kb_ref, vb_ref, gb_ref, gtb_ref, hb_ref))):
        mask = (coli >= rowi) if d else (coli <= rowi)
        tri = jnp.where(mask, 1.0, 0.0)
        g_all = g_ref[0] + br_ref[...]
        gt_all = gt_ref[0] + bc_ref[...]
        bcol_all = jnp.dot(tri, _log_sigmoid(g_all), precision=hp, preferred_element_type=f32)
        brow_all = lax.dot_general(_log_sigmoid(gt_all), tri, (((1,), (1,)), ((), ())), precision=hp,
                                   preferred_element_type=f32)
        for h in range(ML_HEADS):
            ii = 4 * d + h
            fi = 8 + 4 * d + h
            units.append(dict(
                d=d, h=h, mask=mask, out_ref=out_ref,
                i_col=g_all[:, ii:ii + 1], i_row=gt_all[ii:ii + 1, :],
                b_col=bcol_all[:, fi:fi + 1], b_row=brow_all[fi:fi + 1, :],
                q=q_ref[0, :, h * ML_DQK:(h + 1) * ML_DQK] * scale,
                k=k_ref[0, :, h * ML_DQK:(h + 1) * ML_DQK],
                v=v_ref[0, :, h * ML_DV:(h + 1) * ML_DV],
                c=c_ref[d, h], n=n_ref[d, h], m=m_ref[d, h]))
    for u in units:
        logw = jnp.where(u['mask'], u['b_col'] - u['b_row'] + u['i_row'], -jnp.inf)
        m_inter = u['b_col'] + u['m']
        u['m_t'] = jnp.maximum(m_inter, jnp.max(logw, axis=1, keepdims=True))
        u['dm'] = jnp.exp(logw - u['m_t'])
        u['inter'] = jnp.exp(m_inter - u['m_t'])
    qk = [_dot_nt(u['q'], u['k']) for u in units]
    s = [qk[i] * u['dm'] for i, u in enumerate(units)]
    sv = [_dot(s[i], u['v']) for i, u in enumerate(units)]
    qc = [_dot(u['q'], u['c']) for u in units]
    for i, u in enumerate(units):
        den = (jnp.sum(s[i], axis=1, keepdims=True)
               + u['inter'] * jnp.sum(u['q'] * u['n'], axis=1, keepdims=True))
        hh = (sv[i] + u['inter'] * qc[i]) / jnp.maximum(jnp.abs(den), jnp.exp(-u['m_t']))
        u['out_ref'][0, :, u['h'] * ML_DV:(u['h'] + 1) * ML_DV] = hh
        b_last = u['b_col'][0:1] if u['d'] else u['b_col'][L - 1:L]
        logu_col = b_last - u['b_col'] + u['i_col']
        logu_row = b_last - u['b_row'] + u['i_row']
        u['m_new'] = jnp.maximum(b_last + u['m'], jnp.max(logu_row, axis=1, keepdims=True))
        u['decay'] = jnp.exp(b_last + u['m'] - u['m_new'])
        u['ku'] = u['k'] * jnp.exp(logu_col - u['m_new'])
    kv = [_dot_tn(u['ku'], u['v']) for u in units]
    for i, u in enumerate(units):
        d, h = u['d'], u['h']
        c_ref[d, h] = u['decay'] * u['c'] + kv[i]
        n_ref[d, h] = u['decay'] * u['n'] + jnp.sum(u['ku'], axis=0, keepdims=True)
        m_ref[d, h] = u['m_new']


def _mlstm(proj3, gates_t, bias_row, bias_col):
    B, T, _ = proj3.shape
    L = ML_CHUNK
    nc = T // L

    def specs(cm):
        return [pl.BlockSpec((1, L, ML_QK_W), lambda b, c: (b, cm(c), 0)),
                pl.BlockSpec((1, L, ML_QK_W), lambda b, c: (b, cm(c), 1)),
                pl.BlockSpec((1, L, ML_V_W), lambda b, c: (b, cm(c), 1)),
                pl.BlockSpec((1, L, LANES), lambda b, c: (b, cm(c), (8 * D_MODEL) // LANES)),
                pl.BlockSpec((1, 16, L), lambda b, c: (b, 0, cm(c)))]

    fwd = lambda c: c
    bwd = lambda c: nc - 1 - c
    out = jax.ShapeDtypeStruct((B, T, ML_V_W), f32)
    return pl.pallas_call(
        _mlstm_body,
        grid=(B, nc),
        in_specs=[pl.BlockSpec((1, LANES), lambda b, c: (0, 0)), pl.BlockSpec((16, 1), lambda b, c: (0, 0))]
        + specs(fwd) + specs(bwd),
        out_specs=[pl.BlockSpec((1, L, ML_V_W), lambda b, c: (b, c, 0)),
                   pl.BlockSpec((1, L, ML_V_W), lambda b, c: (b, bwd(c), 0))],
        out_shape=[out, out],
        scratch_shapes=[pltpu.VMEM((2, ML_HEADS, ML_DQK, ML_DV), f32),
                        pltpu.VMEM((2, ML_HEADS, 1, ML_DQK), f32),
                        pltpu.VMEM((2, ML_HEADS, 1, 1), f32)],
        compiler_params=_cparams(("parallel", "arbitrary")),
        name="mlstm",
    )(bias_row, bias_col, *([proj3] * 4), gates_t, *([proj3] * 4), gates_t)


def _shift(p, prev8, next8, mu, first, last):
    tt = p.shape[0]
    rows = lax.broadcasted_iota(i32, p.shape, 0)
    prev_row = jnp.where(first, 0.0, prev8[SUBLANES - 1:SUBLANES])
    next_row = jnp.where(last, 0.0, next8[0:1])
    pm = jnp.where(rows == 0, prev_row, pltpu.roll(p, 1, axis=0))
    nx = jnp.where(rows == tt - 1, next_row, pltpu.roll(p, tt - 1, axis=0))
    return p + mu * (0.5 * (pm + nx) - p)


def _rw_prep_body(r_ref, rp_ref, rn_ref, k_ref, kp_ref, kn_ref, v_ref, vp_ref, vn_ref,
                  l_ref, lp_ref, ln_ref, mur_ref, muk_ref, muv_ref, mul_ref,
                  w0_ref, w2h_ref, w2l_ref, a0_ref, a2_ref, g2_ref,
                  ro_ref, ko_ref, vo_ref, ao_ref, lwf_ref, lwb_ref, go_ref):
    i = pl.program_id(1)
    first = i == 0
    last = i == pl.num_programs(1) - 1
    lo = _shift(l_ref[0], lp_ref[0], ln_ref[0], mul_ref[...], first, last)
    ad = lo[:, 256:320]
    sg = _sigmoid(lo[:, 320:448])
    th = [jnp.tanh(lo[:, 128:192]), jnp.tanh(lo[:, 192:256])]
    th_hi = [t.astype(bf16) for t in th]
    th_lo = [(t - h.astype(f32)).astype(bf16) for t, h in zip(th, th_hi)]

    def mm(a, b):
        return jnp.dot(a, b, preferred_element_type=f32)

    CW = 2 * LANES
    for cb in range(RW_W // CW):
        sl = slice(cb * CW, (cb + 1) * CW)
        cols = []
        for src, prv, nxt, mu in ((r_ref, rp_ref, rn_ref, mur_ref), (k_ref, kp_ref, kn_ref, muk_ref),
                                  (v_ref, vp_ref, vn_ref, muv_ref)):
            cols.append(_shift(src[0, :, sl], prv[0, :, sl], nxt[0, :, sl], mu[:, sl], first, last))
        lws = []
        for d in range(2):
            wl = w0_ref[d:d + 1, sl] + (mm(th_hi[d], w2h_ref[d, :, sl]) + mm(th_hi[d], w2l_ref[d, :, sl])
                                        + mm(th_lo[d], w2h_ref[d, :, sl]))
            lws.append(-jnp.exp(-_softplus(-wl) - 0.5))
        a = _sigmoid(a0_ref[:, sl] + _dot(ad, a2_ref[:, sl]))
        go_ref[0, :, sl] = _dot(sg, g2_ref[:, sl])
        for j in range(2):
            hp = 2 * cb + j
            half = slice(j * LANES, (j + 1) * LANES)
            ro_ref[0, hp] = cols[0][:, half]
            ko_ref[0, hp] = cols[1][:, half]
            vo_ref[0, hp] = cols[2][:, half]
            ao_ref[0, hp] = a[:, half]
            lwf_ref[0, hp] = lws[0][:, half]
            lwb_ref[0, hp] = lws[1][:, half]


def _rw_prep(proj3, mu_r, mu_k, mu_v, mu_l, w0, w2h, w2l, a0, a2, g2):
    B, T, _ = proj3.shape
    tt = min(256, T)
    nt = T // tt
    g8 = tt // SUBLANES
    n8 = T // SUBLANES

    def tile(width, blk):
        return [pl.BlockSpec((1, tt, width), lambda b, i: (b, i, blk)),
                pl.BlockSpec((1, SUBLANES, width), lambda b, i: (b, jnp.maximum(i * g8 - 1, 0), blk)),
                pl.BlockSpec((1, SUBLANES, width), lambda b, i: (b, jnp.minimum((i + 1) * g8, n8 - 1), blk))]

    full = lambda shape: pl.BlockSpec(shape, lambda b, i: (0,) * len(shape))
    in_specs = (tile(D_MODEL, 3) + tile(D_MODEL, 4) + tile(D_MODEL, 5) + tile(512, SMALL_BLK)
                + [full((1, RW_W))] * 3 + [full((1, 512)), full((2, RW_W)), full((2, 64, RW_W)), full((2, 64, RW_W)),
                                           full((1, RW_W)), full((64, RW_W)), full((128, RW_W))])
    hm = jax.ShapeDtypeStruct((B, RW_PAIRS, T, LANES), f32)
    hm_spec = pl.BlockSpec((1, RW_PAIRS, tt, LANES), lambda b, i: (b, 0, i, 0))
    return pl.pallas_call(
        _rw_prep_body,
        grid=(B, nt),
        in_specs=in_specs,
        out_specs=[hm_spec] * 6 + [pl.BlockSpec((1, tt, RW_W), lambda b, i: (b, i, 0))],
        out_shape=[hm] * 6 + [jax.ShapeDtypeStruct((B, T, RW_W), f32)],
        compiler_params=_cparams(("parallel", "parallel")),
        name="rw_prep",
    )(*([proj3] * 12), mu_r, mu_k, mu_v, mu_l, w0, w2h, w2l, a0, a2, g2)


def _chunk_cumsum(x, reverse):
    L = x.shape[0]
    rows = lax.broadcasted_iota(i32, x.shape, 0)
    s = 1
    while s < L:
        if reverse:
            x = x + jnp.where(rows < L - s, pltpu.roll(x, L - s, axis=0), 0.0)
        else:
            x = x + jnp.where(rows >= s, pltpu.roll(x, s, axis=0), 0.0)
        s *= 2
    return x


def _block_diag(x, lo):
    return jnp.concatenate([jnp.where(lo, x, 0.0), jnp.where(lo, 0.0, x)], axis=0)


def _rw_prepare(units, lo):
    L = units[0][0].shape[0]
    prepared = []
    for r, kp, v, kk, bh, lw, cum, reverse in units:
        tot = cum[0:1] if reverse else cum[L - 1:L]
        gi = jnp.exp(-cum)
        gl = jnp.exp(tot - cum)
        lhs = jnp.concatenate([-kk * jnp.exp(cum - lw), r * jnp.exp(cum)], axis=0)
        rhs = jnp.concatenate([_block_diag(bh * gi, lo), _block_diag(kp * gi, lo)], axis=0)
        tail = jnp.concatenate([bh * gl, kp * gl], axis=0)
        prepared.append((lhs.astype(bf16), rhs.astype(bf16), tail.astype(bf16),
                         _block_diag(v, lo).astype(bf16), v.astype(bf16), jnp.exp(tot)))
    return prepared


def _rw_solve(prepared, states, reverses, rowi, coli, lo):
    n = len(prepared)
    L = prepared[0][4].shape[0]
    G = [_dot_nt(prepared[i][0], prepared[i][1]) for i in range(n)]
    tok = yield None
    sv = [_dot_nt(prepared[i][0], states[i]) for i in range(n)]
    sv[0] = _anchor(sv[0], tok)
    tok = yield None
    p, a_rb, low = [], [], []
    for i in range(n):
        strict = (coli > rowi) if reverses[i] else (coli < rowi)
        incl = (coli >= rowi) if reverses[i] else (coli <= rowi)
        p.append(jnp.where(strict, G[i][:L, :2 * L], 0.0))
        a_rb.append(jnp.where(incl, G[i][L:, :2 * L], 0.0))
        low.append(jnp.concatenate([jnp.where(strict, G[i][:L, 2 * L:], 0.0),
                                    jnp.where(incl, G[i][L:, 2 * L:], 0.0)], axis=0))
    p[0] = _anchor(p[0], tok)
    av = [_dot(low[i], prepared[i][3]) for i in range(n)]
    tok = yield None
    y = [sv[i][:L] + av[i][:L] for i in range(n)]
    y[0] = _anchor(y[0], tok)
    s = 1
    while 2 * s < L:
        both = [_dot(p[i], jnp.concatenate([_block_diag(y[i], lo), _block_diag(p[i], lo)], axis=1))
                for i in range(n)]
        y = [y[i] + both[i][:, :2 * L] for i in range(n)]
        p = [both[i][:, 2 * L:] for i in range(n)]
        y[0] = _anchor(y[0], (yield None))
        s *= 2
    y = [y[i] + _dot(p[i], _block_diag(y[i], lo)) for i in range(n)]
    y[0] = _anchor(y[0], (yield None))
    ru = [_dot(a_rb[i], _block_diag(y[i], lo)) for i in range(n)]
    y[0] = _anchor(y[0], (yield None))
    upd = [_dot_tn(jnp.concatenate([y[i].astype(bf16), prepared[i][4]], axis=0), prepared[i][2])
           for i in range(n)]
    outs = [sv[i][L:] + av[i][L:] + ru[i] for i in range(n)]
    new_states = [states[i] * prepared[i][5] + upd[i] for i in range(n)]
    yield outs, new_states


def _zero_of(x):
    u = pltpu.bitcast(x[0:SUBLANES].astype(f32), jnp.uint32)
    z = lax.shift_right_logical(lax.shift_right_logical(u, jnp.uint32(16)), jnp.uint32(16))
    return pltpu.bitcast(z, f32)[0:1]


def _anchor(x, tok):
    return x if tok is None else x + tok


def _rw_scan_body(rf_ref, kf_ref, vf_ref, af_ref, lwf_ref, rb_ref, kb_ref, vb_ref, ab_ref, lwb_ref,
                  kk_ref, ka_ref, of_ref, ob_ref, s_ref, *slots):
    L = RW_CHUNK
    H = RW_HEAD
    i = pl.program_id(1)
    names = 6
    slot_a, slot_b = slots[:names], slots[names:]

    @pl.when(i == 0)
    def _():
        s_ref[...] = jnp.zeros_like(s_ref)
        for ref in slots:
            ref[...] = jnp.zeros_like(ref)

    rowi = lax.broadcasted_iota(i32, (L, 2 * L), 0)
    coli = lax.broadcasted_iota(i32, (L, 2 * L), 1) % L
    lo = lax.broadcasted_iota(i32, (1, LANES), 1) < H
    diag = (lax.broadcasted_iota(i32, (LANES, LANES), 0) < H) == (lax.broadcasted_iota(i32, (LANES, LANES), 1) < H)
    dirs = ((rf_ref, kf_ref, vf_ref, af_ref, lwf_ref), (rb_ref, kb_ref, vb_ref, ab_ref, lwb_ref))

    def prepare_pair(hp, dst):
        units = []
        k_k = kk_ref[hp]
        k_a = ka_ref[hp]
        for d, (r_ref, k_ref, v_ref, a_ref, lw_ref) in enumerate(dirs):
            k = k_ref[0, hp]
            a = a_ref[0, hp]
            lw = lw_ref[0, hp]
            cum = _chunk_cumsum(lw, d == 1)
            kk = k * k_k
            sq = kk * kk
            n0 = jnp.sum(jnp.where(lo, sq, 0.0), axis=-1, keepdims=True)
            n1 = jnp.sum(jnp.where(lo, 0.0, sq), axis=-1, keepdims=True)
            kk = kk / jnp.maximum(jnp.sqrt(jnp.where(lo, n0, n1)), L2_EPS)
            kp = k * (1.0 + (a - 1.0) * k_a)
            units.append((r_ref[0, hp], kp, v_ref[0, hp], kk, kk * a, lw, cum, d == 1))
        tok = None
        for d, ops in enumerate(_rw_prepare(units, lo)):
            for ref, val in zip(dst, ops):
                ref[2 * hp + d] = val
            tok = _zero_of(ops[2]) if tok is None else tok + _zero_of(ops[2])
        return tok

    def step(src, dst):
        prepared, states, reverses = [], [], []
        for hp in range(RW_PAIRS):
            for d in range(2):
                prepared.append(tuple(ref[2 * hp + d] for ref in src))
                states.append(s_ref[d, hp])
                reverses.append(d == 1)
        todo = list(range(RW_PAIRS))
        solve = _rw_solve(prepared, states, reverses, rowi, coli, lo)
        result = next(solve)
        while result is None:
            result = solve.send(prepare_pair(todo.pop(0), dst) if todo else None)
        for hp in todo:
            prepare_pair(hp, dst)
        outs, new_states = result
        for hp in range(RW_PAIRS):
            for d, o_ref in enumerate((of_ref, ob_ref)):
                o_ref[0, hp] = outs[2 * hp + d]
                s_ref[d, hp] = jnp.where(diag, new_states[2 * hp + d], 0.0)

    @pl.when(i % 2 == 0)
    def _():
        step(slot_a, slot_b)

    @pl.when(i % 2 == 1)
    def _():
        step(slot_b, slot_a)


def _rw_scan(r, k, v, a, lwf, lwb, k_k, k_a):
    B, _, T, _ = r.shape
    L = RW_CHUNK
    nc = T // L
    last = nc - 1
    fin = pl.BlockSpec((1, RW_PAIRS, L, LANES), lambda b, i: (b, 0, jnp.minimum(i, last), 0))
    bin_ = pl.BlockSpec((1, RW_PAIRS, L, LANES), lambda b, i: (b, 0, last - jnp.minimum(i, last), 0))
    fout = pl.BlockSpec((1, RW_PAIRS, L, LANES), lambda b, i: (b, 0, jnp.maximum(i - 1, 0), 0))
    bout = pl.BlockSpec((1, RW_PAIRS, L, LANES), lambda b, i: (b, 0, last - jnp.maximum(i - 1, 0), 0))
    ps = pl.BlockSpec((RW_PAIRS, 1, LANES), lambda b, i: (0, 0, 0))
    hm = jax.ShapeDtypeStruct((B, RW_PAIRS, T, LANES), f32)
    nu = 2 * RW_PAIRS
    slot = [pltpu.VMEM((nu, 2 * L, LANES), bf16), pltpu.VMEM((nu, 4 * L, LANES), bf16),
            pltpu.VMEM((nu, 2 * L, LANES), bf16), pltpu.VMEM((nu, 2 * L, LANES), bf16),
            pltpu.VMEM((nu, L, LANES), bf16), pltpu.VMEM((nu, 1, LANES), f32)]
    return pl.pallas_call(
        _rw_scan_body,
        grid=(B, nc + 1),
        in_specs=[fin] * 5 + [bin_] * 5 + [ps, ps],
        out_specs=[fout, bout],
        out_shape=[hm, hm],
        scratch_shapes=[pltpu.VMEM((2, RW_PAIRS, LANES, LANES), f32)] + slot + slot,
        compiler_params=_cparams(("parallel", "arbitrary")),
        name="rw_scan",
    )(r, k, v, a, lwf, r, k, v, a, lwb, k_k, k_a)


def _head_mean(x, avg):
    hi = x.astype(bf16)
    lo = (x - hi.astype(f32)).astype(bf16)
    return (jnp.dot(hi, avg, preferred_element_type=f32) + jnp.dot(lo, avg, preferred_element_type=f32))


def _rw_post_body(of_ref, ob_ref, r_ref, k_ref, v_ref, a_ref, g_ref, ka_ref, rk_ref, gg_ref, gb_ref, o_ref):
    same_head = (lax.broadcasted_iota(i32, (LANES, LANES), 0) < RW_HEAD) == (
        lax.broadcasted_iota(i32, (LANES, LANES), 1) < RW_HEAD)
    avg = jnp.where(same_head, 1.0 / RW_HEAD, 0.0).astype(bf16)
    for hp in range(RW_PAIRS):
        out = of_ref[0, hp] + ob_ref[0, hp]
        kp = k_ref[0, hp] * (1.0 + (a_ref[0, hp] - 1.0) * ka_ref[hp])
        rk = r_ref[0, hp] * kp * rk_ref[hp]
        cen = out - _head_mean(out, avg)
        var = _head_mean(cen * cen, avg)
        normed = cen * lax.rsqrt(var + RW_GN_EPS)
        bonus = (_head_mean(rk, avg) * RW_HEAD) * v_ref[0, hp]
        lanes = slice(hp * LANES, (hp + 1) * LANES)
        o_ref[0, :, lanes] = (normed * gg_ref[hp] + gb_ref[hp] + bonus) * g_ref[0, :, lanes]


def _rw_post(out_f, out_b, r, k, v, a, g, k_a, r_k, gn_g, gn_b):
    B, _, T, _ = r.shape
    tt = min(256, T)
    hs = pl.BlockSpec((1, RW_PAIRS, tt, LANES), lambda b, i: (b, 0, i, 0))
    ps = pl.BlockSpec((RW_PAIRS, 1, LANES), lambda b, i: (0, 0, 0))
    ts = pl.BlockSpec((1, tt, RW_W), lambda b, i: (b, i, 0))
    return pl.pallas_call(
        _rw_post_body,
        grid=(B, T // tt),
        in_specs=[hs] * 6 + [ts] + [ps] * 4,
        out_specs=ts,
        out_shape=jax.ShapeDtypeStruct((B, T, RW_W), f32),
        compiler_params=_cparams(("parallel", "parallel")),
        name="rw_post",
    )(out_f, out_b, r, k, v, a, g, k_a, r_k, gn_g, gn_b)


def _merge_body(hf_ref, hb_ref, o_ref, ng_ref, rw_ref, ga_ref, gb_ref, x_ref, wa_ref, wb_ref, wo_ref, nf_ref,
                wr_ref, br_ref, x1_ref, xn_ref, lg_ref):
    heads = []
    for h in range(ML_HEADS):
        sl = slice(h * ML_DV, (h + 1) * ML_DV)
        t = hf_ref[:, sl] + hb_ref[:, sl]
        t = t * lax.rsqrt(jnp.mean(t * t, axis=-1, keepdims=True) + NORM_EPS)
        heads.append((_sigmoid(o_ref[:, sl]) * (t * ng_ref[:, sl])).astype(bf16))
    ml = jnp.concatenate(heads, axis=1)
    y_a = jnp.dot(ml, wa_ref[...], preferred_element_type=f32)
    y_b = jnp.dot(rw_ref[...].astype(bf16), wb_ref[...], preferred_element_type=f32)
    merged = _sigmoid(ga_ref[...]) * y_a + _sigmoid(gb_ref[...]) * y_b
    x1 = x_ref[...] + jnp.dot(merged.astype(bf16), wo_ref[...], preferred_element_type=f32)
    x1_ref[...] = x1
    xn = x1 * lax.rsqrt(jnp.mean(x1 * x1, axis=-1, keepdims=True) + NORM_EPS) * nf_ref[...]
    xn_ref[...] = xn
    wr = wr_ref[...]
    wr_hi = wr.astype(bf16)
    wr_lo = (wr - wr_hi.astype(f32)).astype(bf16)
    xn_hi = xn.astype(bf16)
    xn_lo = (xn - xn_hi.astype(f32)).astype(bf16)
    lg_ref[...] = (jnp.dot(xn_hi, wr_hi, preferred_element_type=f32)
                   + jnp.dot(xn_hi, wr_lo, preferred_element_type=f32)
                   + jnp.dot(xn_lo, wr_hi, preferred_element_type=f32)) + br_ref[...]


def _merge(h_f, h_b, ml_norm_g, rw_out, proj, x2d, w_up_a, w_up_b, w_out, norm_ffn, w_router, b_router):
    n = x2d.shape[0]
    tm = min(512, n)
    row = lambda blk: pl.BlockSpec((tm, D_MODEL), lambda i: (i, blk))
    full = lambda shape: pl.BlockSpec(shape, lambda i: (0,) * len(shape))
    return pl.pallas_call(
        _merge_body,
        grid=(n // tm,),
        in_specs=[row(0), row(0), row(2), full((1, ML_V_W)), row(0), row(6), row(7), row(0),
                  full((D_MODEL, D_MODEL)), full((D_MODEL, D_MODEL)), full((D_MODEL, D_MODEL)),
                  full((1, D_MODEL)), full((D_MODEL, LANES)), full((1, LANES))],
        out_specs=[row(0), row(0), pl.BlockSpec((tm, LANES), lambda i: (i, 0))],
        out_shape=[jax.ShapeDtypeStruct((n, D_MODEL), f32), jax.ShapeDtypeStruct((n, D_MODEL), f32),
                   jax.ShapeDtypeStruct((n, LANES), f32)],
        compiler_params=_cparams(("parallel",)),
        name="merge",
    )(h_f, h_b, proj, ml_norm_g, rw_out, proj, proj, x2d, w_up_a, w_up_b, w_out, norm_ffn, w_router, b_router)


def _first_argmax(p, width):
    lane = lax.broadcasted_iota(i32, p.shape, 1)
    top = jnp.max(p, axis=-1, keepdims=True)
    idx = jnp.min(jnp.where(p == top, lane, width), axis=-1, keepdims=True)
    return top, idx, lane


def _route_body(lg_ref, meta_ref, gate_ref, cnt_ref, carry_ref):
    @pl.when(pl.program_id(0) == 0)
    def _():
        carry_ref[...] = jnp.zeros_like(carry_ref)

    lg = lg_ref[...]
    tr = lg.shape[0]
    gl = lg[:, 0:N_GROUPS]
    ge = jnp.exp(gl - jnp.max(gl, axis=-1, keepdims=True))
    p_group = ge / jnp.sum(ge, axis=-1, keepdims=True)
    p_g, g_idx, _ = _first_argmax(p_group, N_GROUPS)
    el = jnp.zeros((tr, EXPERTS_PER_GROUP), f32)
    for g in range(N_GROUPS):
        lo = N_GROUPS + g * EXPERTS_PER_GROUP
        el = el + jnp.where(g_idx == g, 1.0, 0.0) * lg[:, lo:lo + EXPERTS_PER_GROUP]
    ee = jnp.exp(el - jnp.max(el, axis=-1, keepdims=True))
    p_exp = ee / jnp.sum(ee, axis=-1, keepdims=True)
    p1, i1, lane8 = _first_argmax(p_exp, EXPERTS_PER_GROUP)
    p2, i2, _ = _first_argmax(jnp.where(lane8 == i1, -1.0, p_exp), EXPERTS_PER_GROUP)
    psum = p1 + p2
    g1 = p_g * p1 / psum
    g2 = p_g * p2 / psum
    e0 = g_idx * EXPERTS_PER_GROUP + i1
    e1 = g_idx * EXPERTS_PER_GROUP + i2
    lane = lax.broadcasted_iota(i32, (tr, LANES), 1)
    hit0 = lane == e0
    hit1 = lane == e1
    onehot = jnp.where(hit0 | hit1, 1.0, 0.0)
    rowi = lax.broadcasted_iota(i32, (tr, tr), 0)
    coli = lax.broadcasted_iota(i32, (tr, tr), 1)
    before = jnp.where(coli < rowi, 1.0, 0.0)
    prefix = _dot(before, onehot) + carry_ref[...]
    r0 = jnp.sum(jnp.where(hit0, prefix, 0.0), axis=-1, keepdims=True).astype(i32)
    r1 = jnp.sum(jnp.where(hit1, prefix, 0.0), axis=-1, keepdims=True).astype(i32)
    carry = carry_ref[...] + jnp.sum(onehot, axis=0, keepdims=True)
    carry_ref[...] = carry
    cnt_ref[...] = carry.astype(i32)
    meta_ref[...] = jnp.where(lane == 0, e0, jnp.where(lane == 1, e1, jnp.where(lane == 2, r0, r1)))
    gate_ref[...] = jnp.where(lane == 0, g1, g2)


def _route(logits):
    n = logits.shape[0]
    tr = min(512, n)
    return pl.pallas_call(
        _route_body,
        grid=(n // tr,),
        in_specs=[pl.BlockSpec((tr, LANES), lambda i: (i, 0))],
        out_specs=[pl.BlockSpec((tr, LANES), lambda i: (i, 0)), pl.BlockSpec((tr, LANES), lambda i: (i, 0)),
                   pl.BlockSpec((1, LANES), lambda i: (0, 0))],
        out_shape=[jax.ShapeDtypeStruct((n, LANES), i32), jax.ShapeDtypeStruct((n, LANES), f32),
                   jax.ShapeDtypeStruct((1, LANES), i32)],
        scratch_shapes=[pltpu.VMEM((1, LANES), f32)],
        compiler_params=_cparams(("arbitrary",)),
        name="route",
    )(logits)


def _dest_body(meta_ref, ps_ref, o_ref):
    meta = meta_ref[...]
    lane = lax.broadcasted_iota(i32, meta.shape, 1)
    ps = ps_ref[...].astype(f32)
    seg0 = jnp.sum(jnp.where(lane == meta[:, 0:1], ps, 0.0), axis=-1, keepdims=True).astype(i32)
    seg1 = jnp.sum(jnp.where(lane == meta[:, 1:2], ps, 0.0), axis=-1, keepdims=True).astype(i32)
    o_ref[...] = jnp.where(lane == 0, seg0 + meta[:, 2:3], seg1 + meta[:, 3:4])


def _dest(meta, pstart_row):
    n = meta.shape[0]
    tr = min(1024, n)
    return pl.pallas_call(
        _dest_body,
        grid=(n // tr,),
        in_specs=[pl.BlockSpec((tr, LANES), lambda i: (i, 0)), pl.BlockSpec((1, LANES), lambda i: (0, 0))],
        out_specs=pl.BlockSpec((tr, LANES), lambda i: (i, 0)),
        out_shape=jax.ShapeDtypeStruct((n, LANES), i32),
        compiler_params=_cparams(("parallel",)),
        name="moe_dest",
    )(meta, pstart_row)


def _scatter_body(d0_ref, d1_ref, x_ref, buf_in, xs_hbm, sem):
    del buf_in
    ts = d0_ref.shape[0]

    def row_copy(t, dst):
        return pltpu.make_async_copy(x_ref.at[pl.ds(t, 1)], xs_hbm.at[pl.ds(dst, 1)], sem)

    def issue(t, c):
        row_copy(t, d0_ref[t]).start()
        row_copy(t, d1_ref[t]).start()
        return c

    lax.fori_loop(0, ts, issue, 0, unroll=DMA_UNROLL)

    def drain(t, c):
        row_copy(0, 0).wait()
        row_copy(0, 0).wait()
        return c

    lax.fori_loop(0, ts, drain, 0, unroll=DMA_UNROLL)


def _scatter(xn, d0, d1, n_rows):
    n = xn.shape[0]
    ts = min(512, n)
    tok = pl.BlockSpec((ts,), lambda i: (i,), memory_space=pltpu.SMEM)
    anyspec = pl.BlockSpec(memory_space=pl.ANY)
    buf = jnp.zeros((n_rows, D_MODEL), f32)
    return pl.pallas_call(
        _scatter_body,
        grid=(n // ts,),
        in_specs=[tok, tok, pl.BlockSpec((ts, D_MODEL), lambda i: (i, 0)), anyspec],
        out_specs=anyspec,
        scratch_shapes=[pltpu.SemaphoreType.DMA(())],
        out_shape=jax.ShapeDtypeStruct((n_rows, D_MODEL), f32),
        input_output_aliases={3: 0},
        compiler_params=_cparams(("arbitrary",)),
        name="moe_scatter",
    )(d0, d1, xn, buf)


def _experts_body(be_ref, nu_ref, xs_ref, wg_ref, wu_ref, wd_ref, ys_ref):
    del be_ref

    @pl.when(pl.program_id(0) < nu_ref[0])
    def _():
        xb = xs_ref[...].astype(bf16)
        hg = jnp.dot(xb, wg_ref[0].astype(bf16), preferred_element_type=f32)
        hu = jnp.dot(xb, wu_ref[0].astype(bf16), preferred_element_type=f32)
        hb = hg * _sigmoid(hg) * hu
        ys_ref[...] = jnp.dot(hb.astype(bf16), wd_ref[0].astype(bf16), preferred_element_type=f32)

    @pl.when(pl.program_id(0) >= nu_ref[0])
    def _():
        ys_ref[...] = jnp.zeros_like(ys_ref)


def _experts(xs, block_e, n_used, w_gate, w_up, w_down):
    n_rows = xs.shape[0]
    nb = n_rows // MOE_BLOCK
    return pl.pallas_call(
        _experts_body,
        grid_spec=pltpu.PrefetchScalarGridSpec(
            num_scalar_prefetch=2, grid=(nb,),
            in_specs=[pl.BlockSpec((MOE_BLOCK, D_MODEL), lambda i, be, nu: (i, 0)),
                      pl.BlockSpec((1, D_MODEL, D_EXPERT), lambda i, be, nu: (be[i], 0, 0)),
                      pl.BlockSpec((1, D_MODEL, D_EXPERT), lambda i, be, nu: (be[i], 0, 0)),
                      pl.BlockSpec((1, D_EXPERT, D_MODEL), lambda i, be, nu: (be[i], 0, 0))],
            out_specs=pl.BlockSpec((MOE_BLOCK, D_MODEL), lambda i, be, nu: (i, 0))),
        out_shape=jax.ShapeDtypeStruct((n_rows, D_MODEL), f32),
        compiler_params=_cparams(("arbitrary",)),
        name="moe_experts",
    )(block_e, n_used, xs, w_gate, w_up, w_down)


def _combine_body(d0_ref, d1_ref, x1_ref, gate_ref, nf_ref, ys_hbm, o_ref, y0_ref, y1_ref, sem):
    tc = d0_ref.shape[0]

    def row_copy(src, dst_ref, t):
        return pltpu.make_async_copy(ys_hbm.at[pl.ds(src, 1)], dst_ref.at[pl.ds(t, 1)], sem)

    def issue(t, c):
        row_copy(d0_ref[t], y0_ref, t).start()
        row_copy(d1_ref[t], y1_ref, t).start()
        return c

    lax.fori_loop(0, tc, issue, 0, unroll=DMA_UNROLL)

    def drain(t, c):
        row_copy(0, y0_ref, 0).wait()
        row_copy(0, y1_ref, 0).wait()
        return c

    lax.fori_loop(0, tc, drain, 0, unroll=DMA_UNROLL)
    gates = gate_ref[...]
    x2 = x1_ref[...] + gates[:, 0:1] * y0_ref[...] + gates[:, 1:2] * y1_ref[...]
    o_ref[...] = x2 * lax.rsqrt(jnp.mean(x2 * x2, axis=-1, keepdims=True) + NORM_EPS) * nf_ref[...]


def _combine(x1, gates, norm_final, ys, d0, d1):
    n = x1.shape[0]
    tc = min(256, n)
    tok = pl.BlockSpec((tc,), lambda i: (i,), memory_space=pltpu.SMEM)
    return pl.pallas_call(
        _combine_body,
        grid=(n // tc,),
        in_specs=[tok, tok,
                  pl.BlockSpec((tc, D_MODEL), lambda i: (i, 0)),
                  pl.BlockSpec((tc, LANES), lambda i: (i, 0)),
                  pl.BlockSpec((1, D_MODEL), lambda i: (0, 0)),
                  pl.BlockSpec(memory_space=pl.ANY)],
        out_specs=pl.BlockSpec((tc, D_MODEL), lambda i: (i, 0)),
        scratch_shapes=[pltpu.VMEM((tc, D_MODEL), f32), pltpu.VMEM((tc, D_MODEL), f32),
                        pltpu.SemaphoreType.DMA(())],
        out_shape=jax.ShapeDtypeStruct((n, D_MODEL), f32),
        compiler_params=_cparams(("arbitrary",)),
        name="moe_combine",
    )(d0, d1, x1, gates, norm_final, ys)


def _prep_weights(w_in, rw_mu, w_router_group, b_router_group, w_router_expert, b_router_expert):
    rw0 = ML_COLS
    g0 = ML_COLS + RW_COLS
    zeros = lambda c: jnp.zeros((D_MODEL, c), w_in.dtype)
    w_all = jnp.concatenate([
        w_in[:, 0:3072],
        w_in[:, rw0:rw0 + 3072],
        w_in[:, g0:g0 + 2048],
        w_in[:, 3072:3088], zeros(112),
        w_in[:, rw0 + 3072:rw0 + 3392], zeros(64)], axis=1).astype(bf16)
    mu_r = rw_mu[None, 0:1024]
    mu_k = rw_mu[None, 1024:2048]
    mu_v = rw_mu[None, 2048:3072]
    mu_l = jnp.concatenate([jnp.zeros((128,), f32), rw_mu[3072:3392], jnp.zeros((64,), f32)])[None]
    w_router = jnp.concatenate([w_router_group, w_router_expert,
                                jnp.zeros((D_MODEL, LANES - N_GROUPS - N_EXPERTS), f32)], axis=1)
    b_router = jnp.concatenate([b_router_group, b_router_expert,
                                jnp.zeros((LANES - N_GROUPS - N_EXPERTS,), f32)])[None]
    return w_all, mu_r, mu_k, mu_v, mu_l, w_router, b_router


def _pairs(p):
    return p.reshape(RW_PAIRS, 1, LANES)


def _forward(x, norm_mix, w_all, mus, ml_b_i, ml_b_f, ml_norm_g, rw_w0, rw_w2, rw_a0, rw_a2, rw_g2,
             rw_k_k, rw_k_a, rw_r_k, rw_gn_g, rw_gn_b, w_up_a, w_up_b, w_out, norm_ffn,
             w_router, b_router, w_gate, w_up, w_down, norm_final):
    B, T, _ = x.shape
    n = B * T
    x2d = x.reshape(n, D_MODEL)
    proj = _proj(x2d, norm_mix[None], w_all)
    proj3 = proj.reshape(B, T, W_ALL)
    gates_t = jnp.swapaxes(proj3[:, :, 8 * D_MODEL:8 * D_MODEL + 16], 1, 2)
    bias16 = jnp.concatenate([ml_b_i.reshape(-1), ml_b_f.reshape(-1)])
    bias_row = jnp.concatenate([bias16, jnp.zeros((LANES - 16,), f32)])[None]
    h_f, h_b = _mlstm(proj3, gates_t, bias_row, bias16[:, None])
    mu_r, mu_k, mu_v, mu_l = mus
    w2_hi = rw_w2.astype(bf16)
    w2_lo = (rw_w2 - w2_hi.astype(f32)).astype(bf16)
    r, k, v, a, lwf, lwb, g = _rw_prep(proj3, mu_r, mu_k, mu_v, mu_l, rw_w0, w2_hi, w2_lo, rw_a0[None],
                                       rw_a2, rw_g2)
    out_f, out_b = _rw_scan(r, k, v, a, lwf, lwb, _pairs(rw_k_k), _pairs(rw_k_a))
    rw_out = _rw_post(out_f, out_b, r, k, v, a, g, _pairs(rw_k_a), _pairs(rw_r_k), _pairs(rw_gn_g),
                      _pairs(rw_gn_b))
    x1, xn, logits = _merge(h_f.reshape(n, ML_V_W), h_b.reshape(n, ML_V_W), ml_norm_g[None],
                            rw_out.reshape(n, RW_W), proj, x2d,
                            w_up_a, w_up_b, w_out, norm_ffn[None], w_router, b_router)
    meta, gates, counts = _route(logits)
    counts = counts[0, :N_EXPERTS]
    padded = ((counts + MOE_BLOCK - 1) // MOE_BLOCK) * MOE_BLOCK
    pend = jnp.cumsum(padded)
    pstart = (pend - padded).astype(i32)
    nb = (2 * n) // MOE_BLOCK + N_EXPERTS
    block_start = jnp.arange(nb, dtype=i32) * MOE_BLOCK
    block_e = jnp.minimum(jnp.sum(block_start[:, None] >= pend[None, :], axis=1), N_EXPERTS - 1).astype(i32)
    n_used = (pend[-1:] // MOE_BLOCK).astype(i32)
    pstart_row = jnp.concatenate([pstart, jnp.zeros((LANES - N_EXPERTS,), i32)])[None]
    dest = _dest(meta, pstart_row)
    d0, d1 = dest[:, 0], dest[:, 1]
    xs = _scatter(xn, d0, d1, nb * MOE_BLOCK)
    ys = _experts(xs, block_e, n_used, w_gate, w_up, w_down)
    y = _combine(x1, gates, norm_final[None], ys, d0, d1)
    return y.reshape(B, T, D_MODEL)


def kernel(x_prompt, x_sample, norm_mix, w_in, ml_b_i, ml_b_f, ml_norm_g, rw_mu, rw_w0, rw_w2, rw_a0, rw_a2,
           rw_g2, rw_k_k, rw_k_a, rw_r_k, rw_gn_g, rw_gn_b, w_up_a, w_up_b, w_out, norm_ffn,
           w_router_group, b_router_group, w_router_expert, b_router_expert,
           w_expert_gate, w_expert_up, w_expert_down, norm_final):
    w_all, mu_r, mu_k, mu_v, mu_l, w_router, b_router = _prep_weights(
        w_in[0], rw_mu[0], w_router_group[0], b_router_group[0], w_router_expert[0], b_router_expert[0])
    args = (norm_mix[0], w_all, (mu_r, mu_k, mu_v, mu_l), ml_b_i[0], ml_b_f[0], ml_norm_g[0],
            rw_w0[0], rw_w2[0], rw_a0[0], rw_a2[0], rw_g2[0], rw_k_k[0], rw_k_a[0], rw_r_k[0].reshape(-1),
            rw_gn_g[0], rw_gn_b[0], w_up_a[0].astype(bf16), w_up_b[0].astype(bf16), w_out[0].astype(bf16),
            norm_ffn[0], w_router, b_router, w_expert_gate[0], w_expert_up[0], w_expert_down[0], norm_final)
    return (_forward(x_prompt, *args), _forward(x_sample, *args))
```
